```python
import jax, jax.numpy as jnp
from jax import lax
import numpy as np

D_MODEL = 1024
BATCH = 8
SEQ = 2048
DEPTH = 2

GRID_W = 64
CTX_LEN = 256
HEAD_DIM = 64
N_Q_HEADS = 16
N_KV_HEADS = 4
Q_PER_KV = N_Q_HEADS // N_KV_HEADS
ATTN_WIDTH = N_Q_HEADS * HEAD_DIM
KV_WIDTH = N_KV_HEADS * HEAD_DIM
CONV_WIDTH = D_MODEL
CONV_K = 3
N_BRANCH = 2
Q_BLOCK = 128
ROPE_THETA = 10000.0
EPS = 1e-6
ATTN_SCALE = HEAD_DIM ** -0.5
PROJ_SIZES = (ATTN_WIDTH, KV_WIDTH, KV_WIDTH, ATTN_WIDTH, CONV_WIDTH, CONV_WIDTH, CONV_WIDTH, CONV_WIDTH, N_BRANCH * D_MODEL)
PROJ_WIDTH = 2 * ATTN_WIDTH + 2 * KV_WIDTH + 4 * CONV_WIDTH + N_BRANCH * D_MODEL

kernel_name = "hybrid_gqa_shortconv_dit_prefix"


def rms_norm(x, g):
    x32 = x.astype(jnp.float32)
    y = x32 * lax.rsqrt(jnp.mean(x32 * x32, axis=-1, keepdims=True) + EPS)
    return y.astype(x.dtype) * g


def modulation(cond, w_ada, b_ada):
    m = jax.nn.silu(cond) @ w_ada + b_ada
    return jnp.split(m, 3, axis=-1)


def modulate(x, g, shift, scale):
    return rms_norm(x, g) * (1.0 + scale) + shift


def split_proj(p):
    points = []
    acc = 0
    for s in PROJ_SIZES[:-1]:
        acc += s
        points.append(acc)
    return jnp.split(p, points, axis=-1)


def axial_rope(n_tokens, dtype):
    rows = n_tokens // GRID_W
    row = jnp.repeat(jnp.arange(rows), GRID_W).astype(jnp.float32)
    col = jnp.tile(jnp.arange(GRID_W), rows).astype(jnp.float32)
    half = HEAD_DIM // 2
    inv = ROPE_THETA ** (-jnp.arange(0, half, 2, dtype=jnp.float32) / half)
    ang_r = row[:, None] * inv[None, :]
    ang_c = col[:, None] * inv[None, :]
    ang = jnp.concatenate([ang_r, ang_r, ang_c, ang_c], axis=-1)
    return jnp.cos(ang)[:, None, :].astype(dtype), jnp.sin(ang)[:, None, :].astype(dtype)


def rotate_axial(x):
    x1, x2, x3, x4 = jnp.split(x, 4, axis=-1)
    return jnp.concatenate([-x2, x1, -x4, x3], axis=-1)


def apply_rope(x, cos, sin):
    return x * cos + rotate_axial(x) * sin


def gqa(q, k, v):
    b, lq = q.shape[:2]
    qg = q.reshape(b, lq, N_KV_HEADS, Q_PER_KV, HEAD_DIM)
    s = jnp.einsum('bqkgd,btkd->bkgqt', qg, k) * ATTN_SCALE
    p = jax.nn.softmax(s.astype(jnp.float32), axis=-1).astype(v.dtype)
    o = jnp.einsum('bkgqt,btkd->bqkgd', p, v)
    return o.reshape(b, lq, ATTN_WIDTH)


def latent_attention(q, k_all, v_all):
    b, s = q.shape[:2]
    nb = s // Q_BLOCK
    qb = q.reshape(b, nb, Q_BLOCK, N_Q_HEADS, HEAD_DIM).transpose(1, 0, 2, 3, 4)
    o = lax.map(lambda blk: gqa(blk, k_all, v_all), qb)
    return o.transpose(1, 0, 2, 3).reshape(b, s, ATTN_WIDTH)


def project(h, w_in, qn_g, kn_g):
    b, l = h.shape[:2]
    q, k, v, za, xc, bc, cc, zc, gl = split_proj(h @ w_in)
    q = rms_norm(q.reshape(b, l, N_Q_HEADS, HEAD_DIM), qn_g)
    k = rms_norm(k.reshape(b, l, N_KV_HEADS, HEAD_DIM), kn_g)
    v = v.reshape(b, l, N_KV_HEADS, HEAD_DIM)
    return q, k, v, za, xc, bc, cc, zc, gl


def context_kv(h, w_in, kn_g):
    b, l = h.shape[:2]
    kv = h @ w_in[:, ATTN_WIDTH:ATTN_WIDTH + 2 * KV_WIDTH]
    k, v = jnp.split(kv, 2, axis=-1)
    k = rms_norm(k.reshape(b, l, N_KV_HEADS, HEAD_DIM), kn_g)
    return k, v.reshape(b, l, N_KV_HEADS, HEAD_DIM)


def short_conv(u, w, bias):
    y = lax.conv_general_dilated(u, w[:, None, :].astype(u.dtype), window_strides=(1,), padding=((1, 1),),
                                 dimension_numbers=('NWC', 'WIO', 'NWC'), feature_group_count=CONV_WIDTH)
    return y + bias


def merge_output(attn_o, za, xc, bc, cc, zc, gl, conv_w, conv_b, w_attn_br, w_conv_br, b_gate, w_out):
    attn_br = (attn_o * jax.nn.silu(za)) @ w_attn_br
    y = bc * short_conv(cc * xc, conv_w, conv_b)
    conv_br = (y * jax.nn.silu(zc)) @ w_conv_br
    gates = jax.nn.sigmoid(gl.reshape(*gl.shape[:-1], N_BRANCH, D_MODEL) + b_gate)
    merged = gates[..., 0, :] * attn_br + gates[..., 1, :] * conv_br
    return merged @ w_out


def setup_inputs(seed: int = 0) -> dict:
    key = jax.random.key(seed)
    ks = jax.random.split(key, 16)
    f = jnp.float32
    n = lambda k, shape: jax.random.normal(k, shape, dtype=f)
    return {
        "x": n(ks[0], (BATCH, SEQ, D_MODEL)),
        "c": n(ks[1], (BATCH, D_MODEL)),
        "ctx": n(ks[2], (BATCH, CTX_LEN, D_MODEL)),
        "c_ctx": n(ks[3], (D_MODEL,)),
        "norm_g": 1.0 + 0.02 * n(ks[4], (DEPTH, D_MODEL)),
        "w_ada": n(ks[5], (DEPTH, D_MODEL, 3 * D_MODEL)) * (0.5 * D_MODEL ** -0.5),
        "b_ada": 0.02 * n(ks[6], (DEPTH, 3 * D_MODEL)),
        "w_in": n(ks[7], (DEPTH, D_MODEL, PROJ_WIDTH)) * D_MODEL ** -0.5,
        "q_norm_g": 1.0 + 0.02 * n(ks[8], (DEPTH, HEAD_DIM)),
        "k_norm_g": 1.0 + 0.02 * n(ks[9], (DEPTH, HEAD_DIM)),
        "conv_w": n(ks[10], (DEPTH, CONV_K, CONV_WIDTH)) * CONV_K ** -0.5,
        "conv_b": 0.02 * n(ks[11], (DEPTH, CONV_WIDTH)),
        "w_attn_br": n(ks[12], (DEPTH, ATTN_WIDTH, D_MODEL)) * ATTN_WIDTH ** -0.5,
        "w_conv_br": n(ks[13], (DEPTH, CONV_WIDTH, D_MODEL)) * CONV_WIDTH ** -0.5,
        "b_gate": 0.02 * n(ks[14], (DEPTH, N_BRANCH, D_MODEL)),
        "w_out": n(ks[15], (DEPTH, D_MODEL, D_MODEL)) * D_MODEL ** -0.5,
    }


def reference(x, c, ctx, c_ctx, norm_g, w_ada, b_ada, w_in, q_norm_g, k_norm_g, conv_w, conv_b,
              w_attn_br, w_conv_br, b_gate, w_out):
    s = x.shape[1]
    cos, sin = axial_rope(s, x.dtype)
    for l in range(DEPTH):
        last = l == DEPTH - 1
        sh_x, sc_x, g_x = [m[:, None, :] for m in modulation(c, w_ada[l], b_ada[l])]
        sh_c, sc_c, g_c = modulation(c_ctx, w_ada[l], b_ada[l])
        hx = modulate(x, norm_g[l], sh_x, sc_x)
        hc = modulate(ctx, norm_g[l], sh_c, sc_c)

        qx, kx, vx, za, xc, bc, cc, zc, gl = project(hx, w_in[l], q_norm_g[l], k_norm_g[l])
        qx = apply_rope(qx, cos, sin)
        kx = apply_rope(kx, cos, sin)
        if last:
            kc, vc = context_kv(hc, w_in[l], k_norm_g[l])
        else:
            qc, kc, vc, za_c, xc_c, bc_c, cc_c, zc_c, gl_c = project(hc, w_in[l], q_norm_g[l], k_norm_g[l])

        k_all = jnp.concatenate([kc, kx], axis=1)
        v_all = jnp.concatenate([vc, vx], axis=1)
        attn_x = latent_attention(qx, k_all, v_all)
        out_x = merge_output(attn_x, za, xc, bc, cc, zc, gl, conv_w[l], conv_b[l],
                             w_attn_br[l], w_conv_br[l], b_gate[l], w_out[l])
        if not last:
            attn_c = gqa(qc, kc, vc)
            out_c = merge_output(attn_c, za_c, xc_c, bc_c, cc_c, zc_c, gl_c, conv_w[l], conv_b[l],
                                 w_attn_br[l], w_conv_br[l], b_gate[l], w_out[l])
            ctx = ctx + g_c * out_c
        x = x + g_x * out_x
    return x
```

```python
import functools

import numpy as np
import jax
import jax.numpy as jnp
from jax import lax
from jax.experimental import pallas as pl
from jax.experimental.pallas import tpu as pltpu

D_MODEL = 1024
GRID_W = 64
HEAD_DIM = 64
N_Q_HEADS = 16
N_KV_HEADS = 4
ATTN_WIDTH = N_Q_HEADS * HEAD_DIM
KV_WIDTH = N_KV_HEADS * HEAD_DIM
CONV_WIDTH = D_MODEL
ROPE_THETA = 10000.0
EPS = 1e-6
ATTN_SCALE = HEAD_DIM ** -0.5

OFF_Q = 0
OFF_K = OFF_Q + ATTN_WIDTH
OFF_V = OFF_K + KV_WIDTH
OFF_ZA = OFF_V + KV_WIDTH
OFF_XC = OFF_ZA + ATTN_WIDTH
OFF_BC = OFF_XC + CONV_WIDTH
OFF_CC = OFF_BC + CONV_WIDTH
OFF_ZC = OFF_CC + CONV_WIDTH
OFF_GL = OFF_ZC + CONV_WIDTH
PROJ_WIDTH = OFF_GL + 2 * D_MODEL

V7X_LANES = 128
V7X_MXU_COLS = 256
V7X_BF16_SUBLANES = 16
V7X_VMEM_BYTES = 64 * 1024 * 1024
HALO = V7X_BF16_SUBLANES
ROW_TILE = 256
MOD_ROWS = 16
MOD_COLS = 512

BF16 = jnp.bfloat16
F32 = jnp.float32


def _dot(a, b):
    return jnp.dot(a, b, preferred_element_type=F32)


def _silu(x):
    return x * jax.nn.sigmoid(x)


def _modulation_kernel(cond_ref, w_ref, b_ref, o_ref):
    a = _silu(cond_ref[...]).astype(BF16)
    o_ref[0] = _dot(a, w_ref[0].astype(BF16)) + b_ref[0]


def _modulation(cond, w_ada, b_ada):
    depth = w_ada.shape[0]
    n_out = w_ada.shape[2]
    return pl.pallas_call(
        _modulation_kernel,
        grid=(depth, n_out // MOD_COLS),
        in_specs=[
            pl.BlockSpec((MOD_ROWS, D_MODEL), lambda l, j: (0, 0)),
            pl.BlockSpec((1, D_MODEL, MOD_COLS), lambda l, j: (l, 0, j)),
            pl.BlockSpec((1, 1, MOD_COLS), lambda l, j: (l, 0, j)),
        ],
        out_specs=pl.BlockSpec((1, MOD_ROWS, MOD_COLS), lambda l, j: (l, 0, j)),
        out_shape=jax.ShapeDtypeStruct((depth, MOD_ROWS, n_out), F32),
        name="modulation",
    )(cond, w_ada, b_ada.reshape(depth, 1, n_out))


def _head_norm_rope(raw, gain, ones_ref, rope_refs, row0, tm, out_scale):
    sumsq = _dot((raw * raw).astype(BF16), ones_ref[...])
    r = lax.rsqrt(sumsq * (1.0 / HEAD_DIM) + EPS)
    a = raw * gain
    if rope_refs is None:
        return a * (r * out_scale) if out_scale != 1.0 else a * r
    cos_ref, sin_lo_ref, sin_hi_ref = rope_refs
    rows = pl.ds(row0, tm)
    cos, sin_lo, sin_hi = cos_ref[rows, :], sin_lo_ref[rows, :], sin_hi_ref[rows, :]
    parts = []
    for m in range(V7X_MXU_COLS // V7X_LANES):
        c = a[:, m * V7X_LANES:(m + 1) * V7X_LANES]
        up = pltpu.roll(c, V7X_LANES - HEAD_DIM // 4, 1)
        down = pltpu.roll(c, HEAD_DIM // 4, 1)
        parts.append(c * cos + up * sin_lo + down * sin_hi)
    rot = jnp.concatenate(parts, axis=1)
    return rot * (r * out_scale) if out_scale != 1.0 else rot * r


def _dup_heads(chunk):
    swapped = pltpu.roll(chunk, HEAD_DIM, 1)
    lo = lax.broadcasted_iota(jnp.int32, chunk.shape, 1) < HEAD_DIM
    return jnp.where(lo, chunk, swapped), jnp.where(lo, swapped, chunk)


def _projection_kernel(*refs, rope, full, tm, n_tiles):
    if full:
        (x_ref, xp_ref, xn_ref, shift_ref, scale_ref, ng_ref, w_ref, qg_ref, kg_ref, ones_ref,
         cos_ref, sin_lo_ref, sin_hi_ref, cw_ref, cb_ref, wcb_ref, bg_ref,
         q_out, kt_out, v_out, sza_out, g0_out, t1_out, h_scr, u_scr, yz_scr) = refs
    else:
        (x_ref, shift_ref, scale_ref, ng_ref, w_ref, kg_ref, ones_ref,
         cos_ref, sin_lo_ref, sin_hi_ref, kt_out, v_out, h_scr) = refs
    i = pl.program_id(1)
    rope_refs = (cos_ref, sin_lo_ref, sin_hi_ref) if rope else None
    row0 = pl.multiple_of(i * tm, tm)

    mod_gain = ng_ref[...] * (1.0 + scale_ref[0])
    mod_shift = shift_ref[0]

    def modulated(xv):
        ms = jnp.mean(xv * xv, axis=-1, keepdims=True)
        return ((xv * lax.rsqrt(ms + EPS)) * mod_gain + mod_shift).astype(BF16)

    if full:
        h_scr[0:HALO, :] = modulated(xp_ref[0])
        h_scr[HALO:HALO + tm, :] = modulated(x_ref[0])
        h_scr[HALO + tm:, :] = modulated(xn_ref[0])
        h_main = h_scr[HALO:HALO + tm, :]
    else:
        h_scr[...] = modulated(x_ref[0])
        h_main = h_scr[...]

    def w_cols(off, g):
        return w_ref[:, off + g * V7X_MXU_COLS: off + (g + 1) * V7X_MXU_COLS]

    def group(off, g):
        return _dot(h_main, w_cols(off, g))

    k_w = w_cols(OFF_K, 0) if full else w_ref[:, 0:KV_WIDTH]
    v_w = w_cols(OFF_V, 0) if full else w_ref[:, KV_WIDTH:2 * KV_WIDTH]
    k = _head_norm_rope(_dot(h_main, k_w), kg_ref[...], ones_ref, rope_refs, row0, tm, 1.0)
    v = _dot(h_main, v_w)
    for m in range(KV_WIDTH // V7X_LANES):
        lanes = slice(m * V7X_LANES, (m + 1) * V7X_LANES)
        k_even, k_odd = _dup_heads(k[:, lanes])
        kt_out[0, 2 * m] = k_even.T.astype(BF16)
        kt_out[0, 2 * m + 1] = k_odd.T.astype(BF16)
        v_even, v_odd = _dup_heads(v[:, lanes])
        v_out[0, 2 * m] = v_even.astype(BF16)
        v_out[0, 2 * m + 1] = v_odd.astype(BF16)
    if not full:
        return

    n_groups = ATTN_WIDTH // V7X_MXU_COLS
    for g in range(n_groups):
        cols = slice(g * V7X_MXU_COLS, (g + 1) * V7X_MXU_COLS)
        q = _head_norm_rope(group(OFF_Q, g), qg_ref[...], ones_ref, rope_refs, row0, tm, ATTN_SCALE)
        q_out[0, :, cols] = q.astype(BF16)
        sza_out[0, :, cols] = _silu(group(OFF_ZA, g)).astype(BF16)

    h_ext = h_scr[...]
    row = lax.broadcasted_iota(jnp.int32, (tm, V7X_MXU_COLS), 0)
    first_row = jnp.logical_and(row == 0, i == 0)
    last_row = jnp.logical_and(row == tm - 1, i == n_tiles - 1)
    for g in range(CONV_WIDTH // V7X_MXU_COLS):
        cols = slice(g * V7X_MXU_COLS, (g + 1) * V7X_MXU_COLS)
        u_scr[...] = _dot(h_ext, w_cols(OFF_CC, g)) * _dot(h_ext, w_cols(OFF_XC, g))
        u_prev = jnp.where(first_row, 0.0, u_scr[HALO - 1:HALO - 1 + tm, :])
        u_next = jnp.where(last_row, 0.0, u_scr[HALO + 1:HALO + 1 + tm, :])
        conv = (u_prev * cw_ref[0:1, cols] + u_scr[HALO:HALO + tm, :] * cw_ref[1:2, cols]
                + u_next * cw_ref[2:3, cols] + cb_ref[:, cols])
        y = group(OFF_BC, g) * conv
        yz_scr[:, cols] = (y * _silu(group(OFF_ZC, g))).astype(BF16)

    yz = yz_scr[...]
    for g in range(D_MODEL // V7X_MXU_COLS):
        cols = slice(g * V7X_MXU_COLS, (g + 1) * V7X_MXU_COLS)
        g0_out[0, :, cols] = jax.nn.sigmoid(group(OFF_GL, g) + bg_ref[0:1, cols]).astype(BF16)
        gate1 = jax.nn.sigmoid(group(OFF_GL + D_MODEL, g) + bg_ref[1:2, cols])
        t1_out[0, :, cols] = gate1 * _dot(yz, wcb_ref[:, cols])


def _const_spec(shape):
    return pl.BlockSpec(shape, lambda b, i: (0,) * len(shape), pipeline_mode=pl.Buffered(1))


def _projection(x, mod, mod_row, norm_g, w_in, q_gain, k_gain, ones, rope_tabs, conv_w, conv_b, w_conv_br,
                b_gate, *, rope, full):
    bsz, seq, _ = x.shape
    tm = min(ROW_TILE, seq)
    n_tiles = seq // tm
    halo_per_tile = tm // HALO
    n_halo = seq // HALO

    def mod_spec(part):
        if mod_row is None:
            return pl.BlockSpec((1, 1, D_MODEL), lambda b, i: (b, 0, part))
        return pl.BlockSpec((1, 1, D_MODEL), lambda b, i: (mod_row, 0, part))

    x_spec = pl.BlockSpec((1, tm, D_MODEL), lambda b, i: (b, i, 0))
    xp_spec = pl.BlockSpec((1, HALO, D_MODEL), lambda b, i: (b, jnp.maximum(i * halo_per_tile - 1, 0), 0))
    xn_spec = pl.BlockSpec((1, HALO, D_MODEL),
                           lambda b, i: (b, jnp.minimum((i + 1) * halo_per_tile, n_halo - 1), 0))
    tab_spec = _const_spec(rope_tabs[0].shape)
    kt_spec = pl.BlockSpec((1, N_KV_HEADS, V7X_LANES, tm), lambda b, i: (b, 0, 0, i))
    v_spec = pl.BlockSpec((1, N_KV_HEADS, tm, V7X_LANES), lambda b, i: (b, 0, i, 0))
    kt_shape = jax.ShapeDtypeStruct((bsz, N_KV_HEADS, V7X_LANES, seq), BF16)
    v_shape = jax.ShapeDtypeStruct((bsz, N_KV_HEADS, seq, V7X_LANES), BF16)
    row_spec = pl.BlockSpec((1, tm, D_MODEL), lambda b, i: (b, i, 0))
    vmem_limit = V7X_VMEM_BYTES * 7 // 8

    kern = functools.partial(_projection_kernel, rope=rope, full=full, tm=tm, n_tiles=n_tiles)
    if full:
        in_specs = [x_spec, xp_spec, xn_spec, mod_spec(0), mod_spec(1), _const_spec((1, D_MODEL)),
                    _const_spec(w_in.shape), _const_spec(q_gain.shape), _const_spec(k_gain.shape),
                    _const_spec(ones.shape), tab_spec, tab_spec, tab_spec,
                    _const_spec(conv_w.shape), _const_spec(conv_b.shape), _const_spec(w_conv_br.shape),
                    _const_spec(b_gate.shape)]
        args = (x, x, x, mod, mod, norm_g, w_in, q_gain, k_gain, ones, *rope_tabs, conv_w, conv_b, w_conv_br,
                b_gate)
        out_specs = [row_spec, kt_spec, v_spec, row_spec, row_spec, row_spec]
        out_shape = [jax.ShapeDtypeStruct((bsz, seq, ATTN_WIDTH), BF16), kt_shape, v_shape,
                     jax.ShapeDtypeStruct((bsz, seq, ATTN_WIDTH), BF16),
                     jax.ShapeDtypeStruct((bsz, seq, D_MODEL), BF16),
                     jax.ShapeDtypeStruct((bsz, seq, D_MODEL), F32)]
        scratch = [pltpu.VMEM((tm + 2 * HALO, D_MODEL), BF16),
                   pltpu.VMEM((tm + 2 * HALO, V7X_MXU_COLS), F32),
                   pltpu.VMEM((tm, CONV_WIDTH), BF16)]
    else:
        in_specs = [x_spec, mod_spec(0), mod_spec(1), _const_spec((1, D_MODEL)), _const_spec(w_in.shape),
                    _const_spec(k_gain.shape), _const_spec(ones.shape), tab_spec, tab_spec, tab_spec]
        args = (x, mod, mod, norm_g, w_in, k_gain, ones, *rope_tabs)
        out_specs = [kt_spec, v_spec]
        out_shape = [kt_shape, v_shape]
        scratch = [pltpu.VMEM((tm, D_MODEL), BF16)]
    return pl.pallas_call(
        kern,
        grid=(bsz, n_tiles),
        in_specs=in_specs,
        out_specs=out_specs,
        out_shape=out_shape,
        scratch_shapes=scratch,
        compiler_params=pltpu.CompilerParams(dimension_semantics=("arbitrary", "arbitrary"),
                                             vmem_limit_bytes=vmem_limit),
        name="projection_full" if full else "projection_kv",
    )(*args)


def _attention_kernel(q_ref, kt_ref, v_ref, sza_ref, g0_ref, t1_ref, x_ref, gate_ref, wab_ref, wout_ref,
                      o_ref, attn_scr):
    tq = q_ref.shape[1]
    lo = lax.broadcasted_iota(jnp.int32, (tq, V7X_LANES), 1) < HEAD_DIM
    for j in range(ATTN_WIDTH // V7X_LANES):
        lanes = slice(j * V7X_LANES, (j + 1) * V7X_LANES)
        kv = j * (V7X_LANES // HEAD_DIM) // (N_Q_HEADS // N_KV_HEADS)
        q_pair = q_ref[0, :, lanes].astype(F32)
        kt = kt_ref[0, kv]
        vv = v_ref[0, kv]
        outs = []
        for keep in (lo, jnp.logical_not(lo)):
            q_one = jnp.where(keep, q_pair, 0.0).astype(BF16)
            s = _dot(q_one, kt)
            p = jnp.exp(s - jnp.max(s, axis=-1, keepdims=True))
            denom = jnp.sum(p, axis=-1, keepdims=True)
            outs.append(_dot(p.astype(BF16), vv) / denom)
        attn = jnp.where(lo, outs[0], outs[1])
        attn_scr[:, lanes] = (attn * sza_ref[0, :, lanes].astype(F32)).astype(BF16)
    attn_br = _dot(attn_scr[...], wab_ref[...])
    merged = g0_ref[0].astype(F32) * attn_br + t1_ref[0]
    out = _dot(merged.astype(BF16), wout_ref[...])
    o_ref[0] = x_ref[0] + gate_ref[0] * out


def _attention(q, kt, v, sza, g0, t1, x, mod, mod_row, w_attn_br, w_out):
    bsz, seq, _ = x.shape
    tq = min(ROW_TILE, seq)
    n_keys = kt.shape[-1]
    row_spec = pl.BlockSpec((1, tq, D_MODEL), lambda b, i: (b, i, 0))
    if mod_row is None:
        gate_spec = pl.BlockSpec((1, 1, D_MODEL), lambda b, i: (b, 0, 2))
    else:
        gate_spec = pl.BlockSpec((1, 1, D_MODEL), lambda b, i: (mod_row, 0, 2))
    return pl.pallas_call(
        _attention_kernel,
        grid=(bsz, seq // tq),
        in_specs=[row_spec,
                  pl.BlockSpec((1, N_KV_HEADS, V7X_LANES, n_keys), lambda b, i: (b, 0, 0, 0)),
                  pl.BlockSpec((1, N_KV_HEADS, n_keys, V7X_LANES), lambda b, i: (b, 0, 0, 0)),
                  row_spec, row_spec, row_spec, row_spec, gate_spec,
                  _const_spec(w_attn_br.shape), _const_spec(w_out.shape)],
        out_specs=row_spec,
        out_shape=jax.ShapeDtypeStruct(x.shape, F32),
        scratch_shapes=[pltpu.VMEM((tq, ATTN_WIDTH), BF16)],
        compiler_params=pltpu.CompilerParams(dimension_semantics=("arbitrary", "arbitrary"),
                                             vmem_limit_bytes=V7X_VMEM_BYTES * 3 // 4),
        name="attention_merge",
    )(q, kt, v, sza, g0, t1, x, mod, w_attn_br, w_out)


def _rope_tables(n_tokens):
    t = np.arange(n_tokens)
    row = (t // GRID_W).astype(np.float32)
    col = (t % GRID_W).astype(np.float32)
    half = HEAD_DIM // 2
    inv = (np.float32(ROPE_THETA) ** (-np.arange(0, half, 2, dtype=np.float32) / np.float32(half))).astype(np.float32)
    ang_r = row[:, None] * inv[None, :]
    ang_c = col[:, None] * inv[None, :]
    ang = np.concatenate([ang_r, ang_r, ang_c, ang_c], axis=-1).astype(np.float64)
    cos, sin = np.cos(ang), np.sin(ang)
    first = (np.arange(HEAD_DIM) % half) < half // 2
    sin_lo = np.where(first[None, :], -sin, 0.0)
    sin_hi = np.where(first[None, :], 0.0, sin)
    reps = V7X_LANES // HEAD_DIM
    return tuple(jnp.asarray(np.tile(a, (1, reps)), dtype=F32) for a in (cos, sin_lo, sin_hi))


def _identity_rope_tables(n_tokens):
    one = jnp.ones((n_tokens, V7X_LANES), F32)
    zero = jnp.zeros((n_tokens, V7X_LANES), F32)
    return one, zero, zero


def _head_sum_matrix():
    idx = np.arange(V7X_MXU_COLS) // HEAD_DIM
    return jnp.asarray(idx[:, None] == idx[None, :], dtype=BF16)


def kernel(x, c, ctx, c_ctx, norm_g, w_ada, b_ada, w_in, q_norm_g, k_norm_g, conv_w, conv_b, w_attn_br, w_conv_br,
           b_gate, w_out):
    depth = w_ada.shape[0]
    bsz, seq, _ = x.shape
    ctx_len = ctx.shape[1]
    assert bsz + 1 <= MOD_ROWS and seq % ROW_TILE == 0 and ctx_len % HALO == 0 and ctx_len <= ROW_TILE

    cond = jnp.concatenate([c, c_ctx[None, :], jnp.zeros((MOD_ROWS - bsz - 1, D_MODEL), F32)], axis=0)
    mod = _modulation(cond, w_ada, b_ada)
    ctx_row = bsz

    ones = _head_sum_matrix()
    rope_x = _rope_tables(seq)
    rope_c = _identity_rope_tables(ctx_len)
    reps = V7X_MXU_COLS // HEAD_DIM

    for l in range(depth):
        last = l == depth - 1
        mod_l = mod[l].reshape(MOD_ROWS, 1, 3 * D_MODEL)
        w_in_l = w_in[l].astype(BF16)
        ng = norm_g[l].reshape(1, D_MODEL)
        qg = jnp.tile(q_norm_g[l], reps).reshape(1, V7X_MXU_COLS)
        kg = jnp.tile(k_norm_g[l], reps).reshape(1, V7X_MXU_COLS)
        cb = conv_b[l].reshape(1, CONV_WIDTH)
        wcb = w_conv_br[l].astype(BF16)
        wab = w_attn_br[l].astype(BF16)
        wo = w_out[l].astype(BF16)
        shared = dict(conv_w=conv_w[l], conv_b=cb, w_conv_br=wcb, b_gate=b_gate[l])

        qx, ktx, vx, szax, g0x, t1x = _projection(x, mod_l, None, ng, w_in_l, qg, kg, ones, rope_x,
                                                   rope=True, full=True, **shared)
        if last:
            w_kv = w_in_l[:, OFF_K:OFF_K + 2 * KV_WIDTH]
            ktc, vc = _projection(ctx, mod_l, ctx_row, ng, w_kv, None, kg, ones, rope_c,
                                  rope=False, full=False, conv_w=None, conv_b=None, w_conv_br=None, b_gate=None)
        else:
            qc, ktc, vc, szac, g0c, t1c = _projection(ctx, mod_l, ctx_row, ng, w_in_l, qg, kg, ones, rope_c,
                                                       rope=False, full=True, **shared)
        kt_all = jnp.concatenate([ktc, ktx], axis=-1)
        v_all = jnp.concatenate([vc, vx], axis=-2)
        x_new = _attention(qx, kt_all, v_all, szax, g0x, t1x, x, mod_l, None, wab, wo)
        if not last:
            ctx = _attention(qc, ktc, vc, szac, g0c, t1c, ctx, mod_l, ctx_row, wab, wo)
        x = x_new
    return x
```

```python
import functools

import numpy as np
import jax
import jax.numpy as jnp
from jax import lax
from jax.experimental import pallas as pl
from jax.experimental.pallas import tpu as pltpu

D_MODEL = 1024
GRID_W = 64
HEAD_DIM = 64
N_Q_HEADS = 16
N_KV_HEADS = 4
ATTN_WIDTH = N_Q_HEADS * HEAD_DIM
KV_WIDTH = N_KV_HEADS * HEAD_DIM
CONV_WIDTH = D_MODEL
ROPE_THETA = 10000.0
EPS = 1e-6
ATTN_SCALE = HEAD_DIM ** -0.5
SCORE_SCALE = ATTN_SCALE * 1.4426950408889634

OFF_Q = 0
OFF_K = OFF_Q + ATTN_WIDTH
OFF_V = OFF_K + KV_WIDTH
OFF_ZA = OFF_V + KV_WIDTH
OFF_XC = OFF_ZA + ATTN_WIDTH
OFF_BC = OFF_XC + CONV_WIDTH
OFF_CC = OFF_BC + CONV_WIDTH
OFF_ZC = OFF_CC + CONV_WIDTH
OFF_GL = OFF_ZC + CONV_WIDTH
PROJ_WIDTH = OFF_GL + 2 * D_MODEL

V7X_LANES = 128
V7X_MXU_COLS = 256
V7X_BF16_SUBLANES = 16
V7X_VMEM_BYTES = 64 * 1024 * 1024
HALO = V7X_BF16_SUBLANES
ROW_TILE = 256
PV_ROWS = HEAD_DIM + V7X_BF16_SUBLANES
KEY_TILE = V7X_MXU_COLS
SCORE_LOOKAHEAD = 2
MOD_ROWS = 16
MOD_COLS = 512

BF16 = jnp.bfloat16
F32 = jnp.float32


def _dot(a, b):
    return jnp.dot(a, b, preferred_element_type=F32)


def _silu(x):
    return x * jax.nn.sigmoid(x)


def _modulation_kernel(cond_ref, w_ref, b_ref, o_ref):
    a = _silu(cond_ref[...]).astype(BF16)
    o_ref[0] = _dot(a, w_ref[0].astype(BF16)) + b_ref[0]


def _modulation(cond, w_ada, b_ada):
    depth = w_ada.shape[0]
    n_out = w_ada.shape[2]
    return pl.pallas_call(
        _modulation_kernel,
        grid=(depth, n_out // MOD_COLS),
        in_specs=[
            pl.BlockSpec((MOD_ROWS, D_MODEL), lambda l, j: (0, 0)),
            pl.BlockSpec((1, D_MODEL, MOD_COLS), lambda l, j: (l, 0, j)),
            pl.BlockSpec((1, 1, MOD_COLS), lambda l, j: (l, 0, j)),
        ],
        out_specs=pl.BlockSpec((1, MOD_ROWS, MOD_COLS), lambda l, j: (l, 0, j)),
        out_shape=jax.ShapeDtypeStruct((depth, MOD_ROWS, n_out), F32),
        name="modulation",
    )(cond, w_ada, b_ada.reshape(depth, 1, n_out))


def _head_norm_rope(raw, gain, ones_ref, rope_refs, row0, tm, out_scale):
    sumsq = _dot((raw * raw).astype(BF16), ones_ref[...])
    r = lax.rsqrt(sumsq * (1.0 / HEAD_DIM) + EPS)
    a = raw * gain
    if rope_refs is None:
        return a * (r * out_scale) if out_scale != 1.0 else a * r
    cos_ref, sin_lo_ref, sin_hi_ref = rope_refs
    rows = pl.ds(row0, tm)
    cos, sin_lo, sin_hi = cos_ref[rows, :], sin_lo_ref[rows, :], sin_hi_ref[rows, :]
    parts = []
    for m in range(V7X_MXU_COLS // V7X_LANES):
        c = a[:, m * V7X_LANES:(m + 1) * V7X_LANES]
        up = pltpu.roll(c, V7X_LANES - HEAD_DIM // 4, 1)
        down = pltpu.roll(c, HEAD_DIM // 4, 1)
        parts.append(c * cos + up * sin_lo + down * sin_hi)
    rot = jnp.concatenate(parts, axis=1)
    return rot * (r * out_scale) if out_scale != 1.0 else rot * r


def _projection_kernel(*refs, rope, full, tm, n_tiles):
    if full:
        (x_ref, xp_ref, xn_ref, shift_ref, scale_ref, ng_ref, w_ref, qg_ref, kg_ref, ones_ref,
         cos_ref, sin_lo_ref, sin_hi_ref, cw_ref, cb_ref, wcb_ref, bg_ref,
         qt_out, k_out, vt_out, sza_out, g0_out, t1_out, h_scr, u_scr, yz_scr) = refs
    else:
        (x_ref, shift_ref, scale_ref, ng_ref, w_ref, kg_ref, ones_ref,
         cos_ref, sin_lo_ref, sin_hi_ref, k_out, vt_out, h_scr) = refs
    i = pl.program_id(1)
    rope_refs = (cos_ref, sin_lo_ref, sin_hi_ref) if rope else None
    row0 = pl.multiple_of(i * tm, tm)

    mod_gain = ng_ref[...] * (1.0 + scale_ref[0])
    mod_shift = shift_ref[0]

    def modulated(xv):
        ms = jnp.mean(xv * xv, axis=-1, keepdims=True)
        return ((xv * lax.rsqrt(ms + EPS)) * mod_gain + mod_shift).astype(BF16)

    if full:
        h_scr[0:HALO, :] = modulated(xp_ref[0])
        h_scr[HALO:HALO + tm, :] = modulated(x_ref[0])
        h_scr[HALO + tm:, :] = modulated(xn_ref[0])
        h_main = h_scr[HALO:HALO + tm, :]
    else:
        h_scr[...] = modulated(x_ref[0])
        h_main = h_scr[...]

    def w_cols(off, g):
        return w_ref[:, off + g * V7X_MXU_COLS: off + (g + 1) * V7X_MXU_COLS]

    def group(off, g):
        return _dot(h_main, w_cols(off, g))

    k_w = w_cols(OFF_K, 0) if full else w_ref[:, 0:KV_WIDTH]
    v_w = w_cols(OFF_V, 0) if full else w_ref[:, KV_WIDTH:2 * KV_WIDTH]
    k = _head_norm_rope(_dot(h_main, k_w), kg_ref[...], ones_ref, rope_refs, row0, tm, 1.0)
    v = _dot(h_main, v_w)
    for m in range(KV_WIDTH // V7X_LANES):
        lanes = slice(m * V7X_LANES, (m + 1) * V7X_LANES)
        k_pair = k[:, lanes]
        k_out[0, 2 * m] = k_pair[:, :HEAD_DIM].astype(BF16)
        k_out[0, 2 * m + 1] = pltpu.roll(k_pair, HEAD_DIM, 1)[:, :HEAD_DIM].astype(BF16)
        vt_pair = v[:, lanes].T.astype(BF16)
        vt_out[0, 2 * m, 0:HEAD_DIM, :] = vt_pair[:HEAD_DIM]
        vt_out[0, 2 * m + 1, 0:HEAD_DIM, :] = vt_pair[HEAD_DIM:]
    vt_out[0, :, HEAD_DIM:, :] = jnp.ones((N_KV_HEADS, PV_ROWS - HEAD_DIM, tm), BF16)
    if not full:
        return

    n_groups = ATTN_WIDTH // V7X_MXU_COLS
    for g in range(n_groups):
        cols = slice(g * V7X_MXU_COLS, (g + 1) * V7X_MXU_COLS)
        q = _head_norm_rope(group(OFF_Q, g), qg_ref[...], ones_ref, rope_refs, row0, tm, SCORE_SCALE)
        for m in range(V7X_MXU_COLS // V7X_LANES):
            r0 = g * V7X_MXU_COLS + m * V7X_LANES
            qt_out[0, r0:r0 + V7X_LANES, :] = q[:, m * V7X_LANES:(m + 1) * V7X_LANES].T.astype(BF16)
        sza_out[0, :, cols] = _silu(group(OFF_ZA, g)).astype(BF16)

    h_ext = h_scr[...]
    row = lax.broadcasted_iota(jnp.int32, (tm, V7X_MXU_COLS), 0)
    first_row = jnp.logical_and(row == 0, i == 0)
    last_row = jnp.logical_and(row == tm - 1, i == n_tiles - 1)
    for g in range(CONV_WIDTH // V7X_MXU_COLS):
        cols = slice(g * V7X_MXU_COLS, (g + 1) * V7X_MXU_COLS)
        u_scr[...] = _dot(h_ext, w_cols(OFF_CC, g)) * _dot(h_ext, w_cols(OFF_XC, g))
        u_prev = jnp.where(first_row, 0.0, u_scr[HALO - 1:HALO - 1 + tm, :])
        u_next = jnp.where(last_row, 0.0, u_scr[HALO + 1:HALO + 1 + tm, :])
        conv = (u_prev * cw_ref[0:1, cols] + u_scr[HALO:HALO + tm, :] * cw_ref[1:2, cols]
                + u_next * cw_ref[2:3, cols] + cb_ref[:, cols])
        y = group(OFF_BC, g) * conv
        yz_scr[:, cols] = (y * _silu(group(OFF_ZC, g))).astype(BF16)

    yz = yz_scr[...]
    for g in range(D_MODEL // V7X_MXU_COLS):
        cols = slice(g * V7X_MXU_COLS, (g + 1) * V7X_MXU_COLS)
        g0_out[0, :, cols] = jax.nn.sigmoid(group(OFF_GL, g) + bg_ref[0:1, cols]).astype(BF16)
        gate1 = jax.nn.sigmoid(group(OFF_GL + D_MODEL, g) + bg_ref[1:2, cols])
        t1_out[0, :, cols] = gate1 * _dot(yz, wcb_ref[:, cols])


def _const_spec(shape):
    return pl.BlockSpec(shape, lambda b, i: (0,) * len(shape), pipeline_mode=pl.Buffered(1))


def _projection(x, mod, mod_row, norm_g, w_in, q_gain, k_gain, ones, rope_tabs, conv_w, conv_b, w_conv_br,
                b_gate, *, rope, full):
    bsz, seq, _ = x.shape
    tm = min(ROW_TILE, seq)
    n_tiles = seq // tm
    halo_per_tile = tm // HALO
    n_halo = seq // HALO

    def mod_spec(part):
        if mod_row is None:
            return pl.BlockSpec((1, 1, D_MODEL), lambda b, i: (b, 0, part))
        return pl.BlockSpec((1, 1, D_MODEL), lambda b, i: (mod_row, 0, part))

    x_spec = pl.BlockSpec((1, tm, D_MODEL), lambda b, i: (b, i, 0))
    xp_spec = pl.BlockSpec((1, HALO, D_MODEL), lambda b, i: (b, jnp.maximum(i * halo_per_tile - 1, 0), 0))
    xn_spec = pl.BlockSpec((1, HALO, D_MODEL),
                           lambda b, i: (b, jnp.minimum((i + 1) * halo_per_tile, n_halo - 1), 0))
    tab_spec = _const_spec(rope_tabs[0].shape)
    k_spec = pl.BlockSpec((1, N_KV_HEADS, tm, HEAD_DIM), lambda b, i: (b, 0, i, 0))
    vt_spec = pl.BlockSpec((1, N_KV_HEADS, PV_ROWS, tm), lambda b, i: (b, 0, 0, i))
    k_shape = jax.ShapeDtypeStruct((bsz, N_KV_HEADS, seq, HEAD_DIM), BF16)
    vt_shape = jax.ShapeDtypeStruct((bsz, N_KV_HEADS, PV_ROWS, seq), BF16)
    qt_spec = pl.BlockSpec((1, ATTN_WIDTH, tm), lambda b, i: (b, 0, i))
    row_spec = pl.BlockSpec((1, tm, D_MODEL), lambda b, i: (b, i, 0))
    vmem_limit = V7X_VMEM_BYTES * 7 // 8

    kern = functools.partial(_projection_kernel, rope=rope, full=full, tm=tm, n_tiles=n_tiles)
    if full:
        in_specs = [x_spec, xp_spec, xn_spec, mod_spec(0), mod_spec(1), _const_spec((1, D_MODEL)),
                    _const_spec(w_in.shape), _const_spec(q_gain.shape), _const_spec(k_gain.shape),
                    _const_spec(ones.shape), tab_spec, tab_spec, tab_spec,
                    _const_spec(conv_w.shape), _const_spec(conv_b.shape), _const_spec(w_conv_br.shape),
                    _const_spec(b_gate.shape)]
        args = (x, x, x, mod, mod, norm_g, w_in, q_gain, k_gain, ones, *rope_tabs, conv_w, conv_b, w_conv_br,
                b_gate)
        out_specs = [qt_spec, k_spec, vt_spec, row_spec, row_spec, row_spec]
        out_shape = [jax.ShapeDtypeStruct((bsz, ATTN_WIDTH, seq), BF16), k_shape, vt_shape,
                     jax.ShapeDtypeStruct((bsz, seq, ATTN_WIDTH), BF16),
                     jax.ShapeDtypeStruct((bsz, seq, D_MODEL), BF16),
                     jax.ShapeDtypeStruct((bsz, seq, D_MODEL), F32)]
        scratch = [pltpu.VMEM((tm + 2 * HALO, D_MODEL), BF16),
                   pltpu.VMEM((tm + 2 * HALO, V7X_MXU_COLS), F32),
                   pltpu.VMEM((tm, CONV_WIDTH), BF16)]
    else:
        in_specs = [x_spec, mod_spec(0), mod_spec(1), _const_spec((1, D_MODEL)), _const_spec(w_in.shape),
                    _const_spec(k_gain.shape), _const_spec(ones.shape), tab_spec, tab_spec, tab_spec]
        args = (x, mod, mod, norm_g, w_in, k_gain, ones, *rope_tabs)
        out_specs = [k_spec, vt_spec]
        out_shape = [k_shape, vt_shape]
        scratch = [pltpu.VMEM((tm, D_MODEL), BF16)]
    return pl.pallas_call(
        kern,
        grid=(bsz, n_tiles),
        in_specs=in_specs,
        out_specs=out_specs,
        out_shape=out_shape,
        scratch_shapes=scratch,
        compiler_params=pltpu.CompilerParams(dimension_semantics=("arbitrary", "arbitrary"),
                                             vmem_limit_bytes=vmem_limit),
        name="projection_full" if full else "projection_kv",
    )(*args)


def _attention_kernel(qt_ref, k_ref, vt_ref, sza_ref, g0_ref, t1_ref, x_ref, gate_ref, wab_ref, wout_ref,
                      o_ref, attn_scr, st_scr):
    n_keys = k_ref.shape[2]
    n_tiles = n_keys // KEY_TILE
    group = N_Q_HEADS // N_KV_HEADS
    sub = 8
    slots = st_scr.shape[0]

    def score_tile(h, t, m8):
        rows = slice(t * KEY_TILE, (t + 1) * KEY_TILE)
        s = _dot(k_ref[0, h // group, rows, :], qt_ref[0, h * HEAD_DIM:(h + 1) * HEAD_DIM, :])
        st_scr[h % slots, rows, :] = s.astype(BF16)
        for r in range(KEY_TILE // sub):
            piece = s[r * sub:(r + 1) * sub]
            m8 = piece if m8 is None else jnp.maximum(m8, piece)
        return m8

    def pv_tile(h, t, m_row, acc):
        rows = slice(t * KEY_TILE, (t + 1) * KEY_TILE)
        p = jnp.exp2(st_scr[h % slots, rows, :] - m_row)
        part = _dot(vt_ref[0, h // group, :, rows], p)
        return part if acc is None else acc + part

    col_max = {}
    for h in range(min(SCORE_LOOKAHEAD, N_Q_HEADS)):
        m8 = None
        for t in range(n_tiles):
            m8 = score_tile(h, t, m8)
        col_max[h] = jnp.max(m8, axis=0, keepdims=True).astype(BF16)
    halves = []
    for h in range(N_Q_HEADS):
        acc, m8 = None, None
        ahead = h + SCORE_LOOKAHEAD
        for t in range(n_tiles):
            if ahead < N_Q_HEADS:
                m8 = score_tile(ahead, t, m8)
            acc = pv_tile(h, t, col_max[h], acc)
        if ahead < N_Q_HEADS:
            col_max[ahead] = jnp.max(m8, axis=0, keepdims=True).astype(BF16)
        halves.append(acc[:HEAD_DIM] / acc[HEAD_DIM:HEAD_DIM + 1])
        if len(halves) == V7X_LANES // HEAD_DIM:
            j = h // len(halves)
            lanes = slice(j * V7X_LANES, (j + 1) * V7X_LANES)
            attn = jnp.concatenate(halves, axis=0).T
            attn_scr[:, lanes] = (attn * sza_ref[0, :, lanes].astype(F32)).astype(BF16)
            halves = []
    attn_br = _dot(attn_scr[...], wab_ref[...])
    merged = g0_ref[0].astype(F32) * attn_br + t1_ref[0]
    out = _dot(merged.astype(BF16), wout_ref[...])
    o_ref[0] = x_ref[0] + gate_ref[0] * out


def _attention(qt, k, vt, sza, g0, t1, x, mod, mod_row, w_attn_br, w_out):
    bsz, seq, _ = x.shape
    tq = min(ROW_TILE, seq)
    n_keys = k.shape[2]
    row_spec = pl.BlockSpec((1, tq, D_MODEL), lambda b, i: (b, i, 0))
    if mod_row is None:
        gate_spec = pl.BlockSpec((1, 1, D_MODEL), lambda b, i: (b, 0, 2))
    else:
        gate_spec = pl.BlockSpec((1, 1, D_MODEL), lambda b, i: (mod_row, 0, 2))
    return pl.pallas_call(
        _attention_kernel,
        grid=(bsz, seq // tq),
        in_specs=[pl.BlockSpec((1, ATTN_WIDTH, tq), lambda b, i: (b, 0, i)),
                  pl.BlockSpec((1, N_KV_HEADS, n_keys, HEAD_DIM), lambda b, i: (b, 0, 0, 0)),
                  pl.BlockSpec((1, N_KV_HEADS, PV_ROWS, n_keys), lambda b, i: (b, 0, 0, 0)),
                  row_spec, row_spec, row_spec, row_spec, gate_spec,
                  _const_spec(w_attn_br.shape), _const_spec(w_out.shape)],
        out_specs=row_spec,
        out_shape=jax.ShapeDtypeStruct(x.shape, F32),
        scratch_shapes=[pltpu.VMEM((tq, ATTN_WIDTH), BF16), pltpu.VMEM((SCORE_LOOKAHEAD + 1, n_keys, tq), BF16)],
        compiler_params=pltpu.CompilerParams(dimension_semantics=("arbitrary", "arbitrary"),
                                             vmem_limit_bytes=V7X_VMEM_BYTES * 3 // 4),
        name="attention_merge",
    )(qt, k, vt, sza, g0, t1, x, mod, w_attn_br, w_out)


def _rope_tables(n_tokens):
    t = np.arange(n_tokens)
    row = (t // GRID_W).astype(np.float32)
    col = (t % GRID_W).astype(np.float32)
    half = HEAD_DIM // 2
    inv = (np.float32(ROPE_THETA) ** (-np.arange(0, half, 2, dtype=np.float32) / np.float32(half))).astype(np.float32)
    ang_r = row[:, None] * inv[None, :]
    ang_c = col[:, None] * inv[None, :]
    ang = np.concatenate([ang_r, ang_r, ang_c, ang_c], axis=-1).astype(np.float64)
    cos, sin = np.cos(ang), np.sin(ang)
    first = (np.arange(HEAD_DIM) % half) < half // 2
    sin_lo = np.where(first[None, :], -sin, 0.0)
    sin_hi = np.where(first[None, :], 0.0, sin)
    reps = V7X_LANES // HEAD_DIM
    return tuple(jnp.asarray(np.tile(a, (1, reps)), dtype=F32) for a in (cos, sin_lo, sin_hi))


def _identity_rope_tables(n_tokens):
    one = jnp.ones((n_tokens, V7X_LANES), F32)
    zero = jnp.zeros((n_tokens, V7X_LANES), F32)
    return one, zero, zero


def _head_sum_matrix():
    idx = np.arange(V7X_MXU_COLS) // HEAD_DIM
    return jnp.asarray(idx[:, None] == idx[None, :], dtype=BF16)


def kernel(x, c, ctx, c_ctx, norm_g, w_ada, b_ada, w_in, q_norm_g, k_norm_g, conv_w, conv_b, w_attn_br, w_conv_br,
           b_gate, w_out):
    depth = w_ada.shape[0]
    bsz, seq, _ = x.shape
    ctx_len = ctx.shape[1]
    assert bsz + 1 <= MOD_ROWS and seq % ROW_TILE == 0 and ctx_len % HALO == 0 and ctx_len <= ROW_TILE

    cond = jnp.concatenate([c, c_ctx[None, :], jnp.zeros((MOD_ROWS - bsz - 1, D_MODEL), F32)], axis=0)
    mod = _modulation(cond, w_ada, b_ada)
    ctx_row = bsz

    ones = _head_sum_matrix()
    rope_x = _rope_tables(seq)
    rope_c = _identity_rope_tables(ctx_len)
    reps = V7X_MXU_COLS // HEAD_DIM

    for l in range(depth):
        last = l == depth - 1
        mod_l = mod[l].reshape(MOD_ROWS, 1, 3 * D_MODEL)
        w_in_l = w_in[l].astype(BF16)
        ng = norm_g[l].reshape(1, D_MODEL)
        qg = jnp.tile(q_norm_g[l], reps).reshape(1, V7X_MXU_COLS)
        kg = jnp.tile(k_norm_g[l], reps).reshape(1, V7X_MXU_COLS)
        cb = conv_b[l].reshape(1, CONV_WIDTH)
        wcb = w_conv_br[l].astype(BF16)
        wab = w_attn_br[l].astype(BF16)
        wo = w_out[l].astype(BF16)
        shared = dict(conv_w=conv_w[l], conv_b=cb, w_conv_br=wcb, b_gate=b_gate[l])

        qx, ktx, vx, szax, g0x, t1x = _projection(x, mod_l, None, ng, w_in_l, qg, kg, ones, rope_x,
                                                   rope=True, full=True, **shared)
        if last:
            w_kv = w_in_l[:, OFF_K:OFF_K + 2 * KV_WIDTH]
            ktc, vc = _projection(ctx, mod_l, ctx_row, ng, w_kv, None, kg, ones, rope_c,
                                  rope=False, full=False, conv_w=None, conv_b=None, w_conv_br=None, b_gate=None)
        else:
            qc, ktc, vc, szac, g0c, t1c = _projection(ctx, mod_l, ctx_row, ng, w_in_l, qg, kg, ones, rope_c,
                                                       rope=False, full=True, **shared)
        k_all = jnp.concatenate([ktc, ktx], axis=2)
        vt_all = jnp.concatenate([vc, vx], axis=3)
        x_new = _attention(qx, k_all, vt_all, szax, g0x, t1x, x, mod_l, None, wab, wo)
        if not last:
            ctx = _attention(qc, ktc, vc, szac, g0c, t1c, ctx, mod_l, ctx_row, wab, wo)
        x = x_new
    return x
```

```python
import functools

import numpy as np
import jax
import jax.numpy as jnp
from jax import lax
from jax.experimental import pallas as pl
from jax.experimental.pallas import tpu as pltpu

D_MODEL = 1024
GRID_W = 64
HEAD_DIM = 64
N_Q_HEADS = 16
N_KV_HEADS = 4
ATTN_WIDTH = N_Q_HEADS * HEAD_DIM
KV_WIDTH = N_KV_HEADS * HEAD_DIM
CONV_WIDTH = D_MODEL
ROPE_THETA = 10000.0
EPS = 1e-6
ATTN_SCALE = HEAD_DIM ** -0.5
SCORE_SCALE = ATTN_SCALE * 1.4426950408889634

OFF_Q = 0
OFF_K = OFF_Q + ATTN_WIDTH
OFF_V = OFF_K + KV_WIDTH
OFF_ZA = OFF_V + KV_WIDTH
OFF_XC = OFF_ZA + ATTN_WIDTH
OFF_BC = OFF_XC + CONV_WIDTH
OFF_CC = OFF_BC + CONV_WIDTH
OFF_ZC = OFF_CC + CONV_WIDTH
OFF_GL = OFF_ZC + CONV_WIDTH
PROJ_WIDTH = OFF_GL + 2 * D_MODEL

V7X_LANES = 128
V7X_MXU_COLS = 256
V7X_BF16_SUBLANES = 16
V7X_VMEM_BYTES = 64 * 1024 * 1024
HALO = V7X_BF16_SUBLANES
ROW_TILE = 256
PV_ROWS = HEAD_DIM + V7X_BF16_SUBLANES
KEY_TILE = V7X_MXU_COLS
SCORE_LOOKAHEAD = 2
MOD_ROWS = 16
MOD_COLS = 512

BF16 = jnp.bfloat16
F32 = jnp.float32


def _dot(a, b):
    return jnp.dot(a, b, preferred_element_type=F32)


def _silu(x):
    return x * jax.nn.sigmoid(x)


def _modulation_kernel(cond_ref, w_ref, b_ref, o_ref):
    a = _silu(cond_ref[...]).astype(BF16)
    o_ref[0] = _dot(a, w_ref[0].astype(BF16)) + b_ref[0]


def _modulation(cond, w_ada, b_ada):
    depth = w_ada.shape[0]
    n_out = w_ada.shape[2]
    return pl.pallas_call(
        _modulation_kernel,
        grid=(depth, n_out // MOD_COLS),
        in_specs=[
            pl.BlockSpec((MOD_ROWS, D_MODEL), lambda l, j: (0, 0)),
            pl.BlockSpec((1, D_MODEL, MOD_COLS), lambda l, j: (l, 0, j)),
            pl.BlockSpec((1, 1, MOD_COLS), lambda l, j: (l, 0, j)),
        ],
        out_specs=pl.BlockSpec((1, MOD_ROWS, MOD_COLS), lambda l, j: (l, 0, j)),
        out_shape=jax.ShapeDtypeStruct((depth, MOD_ROWS, n_out), F32),
        name="modulation",
    )(cond, w_ada, b_ada.reshape(depth, 1, n_out))


def _head_norm_rope(raw, gain, ones_ref, rope_refs, row0, tm, out_scale):
    sumsq = _dot((raw * raw).astype(BF16), ones_ref[...])
    r = lax.rsqrt(sumsq * (1.0 / HEAD_DIM) + EPS)
    a = raw * gain
    if rope_refs is None:
        return a * (r * out_scale) if out_scale != 1.0 else a * r
    cos_ref, sin_lo_ref, sin_hi_ref = rope_refs
    rows = pl.ds(row0, tm)
    cos, sin_lo, sin_hi = cos_ref[rows, :], sin_lo_ref[rows, :], sin_hi_ref[rows, :]
    parts = []
    for m in range(V7X_MXU_COLS // V7X_LANES):
        c = a[:, m * V7X_LANES:(m + 1) * V7X_LANES]
        up = pltpu.roll(c, V7X_LANES - HEAD_DIM // 4, 1)
        down = pltpu.roll(c, HEAD_DIM // 4, 1)
        parts.append(c * cos + up * sin_lo + down * sin_hi)
    rot = jnp.concatenate(parts, axis=1)
    return rot * (r * out_scale) if out_scale != 1.0 else rot * r


def _projection_kernel(*refs, rope, full, tm, n_tiles):
    if full:
        (x_ref, xp_ref, xn_ref, shift_ref, scale_ref, ng_ref, w_ref, qg_ref, kg_ref, ones_ref,
         cos_ref, sin_lo_ref, sin_hi_ref, cw_ref, cb_ref, wcb_ref, bg_ref,
         qt_out, k_out, vt_out, sza_out, g0_out, t1_out, h_scr, u_scr, yz_scr) = refs
    else:
        (x_ref, shift_ref, scale_ref, ng_ref, w_ref, kg_ref, ones_ref,
         cos_ref, sin_lo_ref, sin_hi_ref, k_out, vt_out, h_scr) = refs
    i = pl.program_id(1)
    rope_refs = (cos_ref, sin_lo_ref, sin_hi_ref) if rope else None
    row0 = pl.multiple_of(i * tm, tm)

    mod_gain = ng_ref[...] * (1.0 + scale_ref[0])
    mod_shift = shift_ref[0]

    def modulated(xv):
        ms = jnp.mean(xv * xv, axis=-1, keepdims=True)
        return ((xv * lax.rsqrt(ms + EPS)) * mod_gain + mod_shift).astype(BF16)

    if full:
        h_scr[0:HALO, :] = modulated(xp_ref[0])
        h_scr[HALO:HALO + tm, :] = modulated(x_ref[0])
        h_scr[HALO + tm:, :] = modulated(xn_ref[0])
        h_main = h_scr[HALO:HALO + tm, :]
    else:
        h_scr[...] = modulated(x_ref[0])
        h_main = h_scr[...]

    def w_cols(off, g):
        return w_ref[:, off + g * V7X_MXU_COLS: off + (g + 1) * V7X_MXU_COLS]

    def group(off, g):
        return _dot(h_main, w_cols(off, g))

    k_w = w_cols(OFF_K, 0) if full else w_ref[:, 0:KV_WIDTH]
    v_w = w_cols(OFF_V, 0) if full else w_ref[:, KV_WIDTH:2 * KV_WIDTH]
    k = _head_norm_rope(_dot(h_main, k_w), kg_ref[...], ones_ref, rope_refs, row0, tm, 1.0)
    v = _dot(h_main, v_w)
    for m in range(KV_WIDTH // V7X_LANES):
        lanes = slice(m * V7X_LANES, (m + 1) * V7X_LANES)
        k_pair = k[:, lanes]
        k_out[0, 2 * m] = k_pair[:, :HEAD_DIM].astype(BF16)
        k_out[0, 2 * m + 1] = pltpu.roll(k_pair, HEAD_DIM, 1)[:, :HEAD_DIM].astype(BF16)
        vt_pair = v[:, lanes].T.astype(BF16)
        vt_out[0, 2 * m, 0:HEAD_DIM, :] = vt_pair[:HEAD_DIM]
        vt_out[0, 2 * m + 1, 0:HEAD_DIM, :] = vt_pair[HEAD_DIM:]
    vt_out[0, :, HEAD_DIM:, :] = jnp.ones((N_KV_HEADS, PV_ROWS - HEAD_DIM, tm), BF16)
    if not full:
        return

    n_groups = ATTN_WIDTH // V7X_MXU_COLS
    for g in range(n_groups):
        cols = slice(g * V7X_MXU_COLS, (g + 1) * V7X_MXU_COLS)
        q = _head_norm_rope(group(OFF_Q, g), qg_ref[...], ones_ref, rope_refs, row0, tm, SCORE_SCALE)
        for m in range(V7X_MXU_COLS // V7X_LANES):
            r0 = g * V7X_MXU_COLS + m * V7X_LANES
            qt_out[0, r0:r0 + V7X_LANES, :] = q[:, m * V7X_LANES:(m + 1) * V7X_LANES].T.astype(BF16)
        sza_out[0, :, cols] = _silu(group(OFF_ZA, g)).astype(BF16)

    h_ext = h_scr[...]
    row = lax.broadcasted_iota(jnp.int32, (tm, V7X_MXU_COLS), 0)
    first_row = jnp.logical_and(row == 0, i == 0)
    last_row = jnp.logical_and(row == tm - 1, i == n_tiles - 1)
    for g in range(CONV_WIDTH // V7X_MXU_COLS):
        cols = slice(g * V7X_MXU_COLS, (g + 1) * V7X_MXU_COLS)
        u_scr[...] = _dot(h_ext, w_cols(OFF_CC, g)) * _dot(h_ext, w_cols(OFF_XC, g))
        u_prev = jnp.where(first_row, 0.0, u_scr[HALO - 1:HALO - 1 + tm, :])
        u_next = jnp.where(last_row, 0.0, u_scr[HALO + 1:HALO + 1 + tm, :])
        conv = (u_prev * cw_ref[0:1, cols] + u_scr[HALO:HALO + tm, :] * cw_ref[1:2, cols]
                + u_next * cw_ref[2:3, cols] + cb_ref[:, cols])
        y = group(OFF_BC, g) * conv
        yz_scr[:, cols] = (y * _silu(group(OFF_ZC, g))).astype(BF16)

    yz = yz_scr[...]
    for g in range(D_MODEL // V7X_MXU_COLS):
        cols = slice(g * V7X_MXU_COLS, (g + 1) * V7X_MXU_COLS)
        g0_out[0, :, cols] = jax.nn.sigmoid(group(OFF_GL, g) + bg_ref[0:1, cols]).astype(BF16)
        gate1 = jax.nn.sigmoid(group(OFF_GL + D_MODEL, g) + bg_ref[1:2, cols])
        t1_out[0, :, cols] = gate1 * _dot(yz, wcb_ref[:, cols])


def _const_spec(shape):
    return pl.BlockSpec(shape, lambda b, i: (0,) * len(shape), pipeline_mode=pl.Buffered(1))


def _projection(x, mod, mod_row, norm_g, w_in, q_gain, k_gain, ones, rope_tabs, conv_w, conv_b, w_conv_br,
                b_gate, *, rope, full):
    bsz, seq, _ = x.shape
    tm = min(ROW_TILE, seq)
    n_tiles = seq // tm
    halo_per_tile = tm // HALO
    n_halo = seq // HALO

    def mod_spec(part):
        if mod_row is None:
            return pl.BlockSpec((1, 1, D_MODEL), lambda b, i: (b, 0, part))
        return pl.BlockSpec((1, 1, D_MODEL), lambda b, i: (mod_row, 0, part))

    x_spec = pl.BlockSpec((1, tm, D_MODEL), lambda b, i: (b, i, 0))
    xp_spec = pl.BlockSpec((1, HALO, D_MODEL), lambda b, i: (b, jnp.maximum(i * halo_per_tile - 1, 0), 0))
    xn_spec = pl.BlockSpec((1, HALO, D_MODEL),
                           lambda b, i: (b, jnp.minimum((i + 1) * halo_per_tile, n_halo - 1), 0))
    tab_spec = _const_spec(rope_tabs[0].shape)
    k_spec = pl.BlockSpec((1, N_KV_HEADS, tm, HEAD_DIM), lambda b, i: (b, 0, i, 0))
    vt_spec = pl.BlockSpec((1, N_KV_HEADS, PV_ROWS, tm), lambda b, i: (b, 0, 0, i))
    k_shape = jax.ShapeDtypeStruct((bsz, N_KV_HEADS, seq, HEAD_DIM), BF16)
    vt_shape = jax.ShapeDtypeStruct((bsz, N_KV_HEADS, PV_ROWS, seq), BF16)
    qt_spec = pl.BlockSpec((1, ATTN_WIDTH, tm), lambda b, i: (b, 0, i))
    row_spec = pl.BlockSpec((1, tm, D_MODEL), lambda b, i: (b, i, 0))
    vmem_limit = V7X_VMEM_BYTES * 7 // 8

    kern = functools.partial(_projection_kernel, rope=rope, full=full, tm=tm, n_tiles=n_tiles)
    if full:
        in_specs = [x_spec, xp_spec, xn_spec, mod_spec(0), mod_spec(1), _const_spec((1, D_MODEL)),
                    _const_spec(w_in.shape), _const_spec(q_gain.shape), _const_spec(k_gain.shape),
                    _const_spec(ones.shape), tab_spec, tab_spec, tab_spec,
                    _const_spec(conv_w.shape), _const_spec(conv_b.shape), _const_spec(w_conv_br.shape),
                    _const_spec(b_gate.shape)]
        args = (x, x, x, mod, mod, norm_g, w_in, q_gain, k_gain, ones, *rope_tabs, conv_w, conv_b, w_conv_br,
                b_gate)
        out_specs = [qt_spec, k_spec, vt_spec, row_spec, row_spec, row_spec]
        out_shape = [jax.ShapeDtypeStruct((bsz, ATTN_WIDTH, seq), BF16), k_shape, vt_shape,
                     jax.ShapeDtypeStruct((bsz, seq, ATTN_WIDTH), BF16),
                     jax.ShapeDtypeStruct((bsz, seq, D_MODEL), BF16),
                     jax.ShapeDtypeStruct((bsz, seq, D_MODEL), F32)]
        scratch = [pltpu.VMEM((tm + 2 * HALO, D_MODEL), BF16),
                   pltpu.VMEM((tm + 2 * HALO, V7X_MXU_COLS), F32),
                   pltpu.VMEM((tm, CONV_WIDTH), BF16)]
    else:
        in_specs = [x_spec, mod_spec(0), mod_spec(1), _const_spec((1, D_MODEL)), _const_spec(w_in.shape),
                    _const_spec(k_gain.shape), _const_spec(ones.shape), tab_spec, tab_spec, tab_spec]
        args = (x, mod, mod, norm_g, w_in, k_gain, ones, *rope_tabs)
        out_specs = [k_spec, vt_spec]
        out_shape = [k_shape, vt_shape]
        scratch = [pltpu.VMEM((tm, D_MODEL), BF16)]
    return pl.pallas_call(
        kern,
        grid=(bsz, n_tiles),
        in_specs=in_specs,
        out_specs=out_specs,
        out_shape=out_shape,
        scratch_shapes=scratch,
        compiler_params=pltpu.CompilerParams(dimension_semantics=("arbitrary", "arbitrary"),
                                             vmem_limit_bytes=vmem_limit),
        name="projection_full" if full else "projection_kv",
    )(*args)


def _attention_kernel(*refs, n_seg):
    slot0_ref, qt_ref = refs[0], refs[1]
    refs = refs[1:]
    k_refs, vt_refs = refs[1:1 + n_seg], refs[1 + n_seg:1 + 2 * n_seg]
    (sza_ref, g0_ref, t1_ref, x_ref, gate_ref, wab_ref, wout_ref, o_ref, attn_scr, st_scr) = refs[1 + 2 * n_seg:]
    slot0 = slot0_ref[0]
    tiles = [(seg, t) for seg in range(n_seg) for t in range(k_refs[seg].shape[2] // KEY_TILE)]
    n_tiles = len(tiles)
    group = N_Q_HEADS // N_KV_HEADS
    sub = 8
    slots = st_scr.shape[0]

    def score_tile(h, t, m8):
        seg, local = tiles[t]
        s = _dot(k_refs[seg][0, h // group, local * KEY_TILE:(local + 1) * KEY_TILE, :],
                 qt_ref[0, h * HEAD_DIM:(h + 1) * HEAD_DIM, :])
        st_scr[slot0 + h % slots, t * KEY_TILE:(t + 1) * KEY_TILE, :] = s.astype(BF16)
        for r in range(KEY_TILE // sub):
            piece = s[r * sub:(r + 1) * sub]
            m8 = piece if m8 is None else jnp.maximum(m8, piece)
        return m8

    def pv_tile(h, t, m_row, acc):
        seg, local = tiles[t]
        p = jnp.exp2(st_scr[slot0 + h % slots, t * KEY_TILE:(t + 1) * KEY_TILE, :] - m_row)
        part = _dot(vt_refs[seg][0, h // group, :, local * KEY_TILE:(local + 1) * KEY_TILE], p)
        return part if acc is None else acc + part

    col_max = {}
    for h in range(min(SCORE_LOOKAHEAD, N_Q_HEADS)):
        m8 = None
        for t in range(n_tiles):
            m8 = score_tile(h, t, m8)
        col_max[h] = jnp.max(m8, axis=0, keepdims=True).astype(BF16)
    halves = []
    for h in range(N_Q_HEADS):
        acc, m8 = None, None
        ahead = h + SCORE_LOOKAHEAD
        for t in range(n_tiles):
            if ahead < N_Q_HEADS:
                m8 = score_tile(ahead, t, m8)
            acc = pv_tile(h, t, col_max[h], acc)
        if ahead < N_Q_HEADS:
            col_max[ahead] = jnp.max(m8, axis=0, keepdims=True).astype(BF16)
        halves.append(acc[:HEAD_DIM] / acc[HEAD_DIM:HEAD_DIM + 1])
        if len(halves) == V7X_LANES // HEAD_DIM:
            j = h // len(halves)
            lanes = slice(j * V7X_LANES, (j + 1) * V7X_LANES)
            attn = jnp.concatenate(halves, axis=0).T
            attn_scr[:, lanes] = (attn * sza_ref[0, :, lanes].astype(F32)).astype(BF16)
            halves = []
    attn_br = _dot(attn_scr[...], wab_ref[...])
    merged = g0_ref[0].astype(F32) * attn_br + t1_ref[0]
    out = _dot(merged.astype(BF16), wout_ref[...])
    o_ref[0] = x_ref[0] + gate_ref[0] * out


def _attention(qt, ks, vts, sza, g0, t1, x, mod, mod_row, w_attn_br, w_out):
    bsz, seq, _ = x.shape
    tq = min(ROW_TILE, seq)
    n_keys = sum(k.shape[2] for k in ks)
    k_specs = [pl.BlockSpec((1, N_KV_HEADS, k.shape[2], HEAD_DIM), lambda b, i: (b, 0, 0, 0)) for k in ks]
    vt_specs = [pl.BlockSpec((1, N_KV_HEADS, PV_ROWS, vt.shape[3]), lambda b, i: (b, 0, 0, 0)) for vt in vts]
    row_spec = pl.BlockSpec((1, tq, D_MODEL), lambda b, i: (b, i, 0))
    if mod_row is None:
        gate_spec = pl.BlockSpec((1, 1, D_MODEL), lambda b, i: (b, 0, 2))
    else:
        gate_spec = pl.BlockSpec((1, 1, D_MODEL), lambda b, i: (mod_row, 0, 2))
    return pl.pallas_call(
        functools.partial(_attention_kernel, n_seg=len(ks)),
        grid=(bsz, seq // tq),
        in_specs=[pl.BlockSpec(memory_space=pltpu.SMEM),
                  pl.BlockSpec((1, ATTN_WIDTH, tq), lambda b, i: (b, 0, i)), *k_specs, *vt_specs,
                  row_spec, row_spec, row_spec, row_spec, gate_spec,
                  _const_spec(w_attn_br.shape), _const_spec(w_out.shape)],
        out_specs=row_spec,
        out_shape=jax.ShapeDtypeStruct(x.shape, F32),
        scratch_shapes=[pltpu.VMEM((tq, ATTN_WIDTH), BF16), pltpu.VMEM((SCORE_LOOKAHEAD + 1, n_keys, tq), BF16)],
        compiler_params=pltpu.CompilerParams(dimension_semantics=("arbitrary", "arbitrary"),
                                             vmem_limit_bytes=V7X_VMEM_BYTES * 3 // 4),
        name="attention_merge",
    )(jnp.zeros((1,), jnp.int32), qt, *ks, *vts, sza, g0, t1, x, mod, w_attn_br, w_out)


def _rope_tables(n_tokens):
    t = np.arange(n_tokens)
    row = (t // GRID_W).astype(np.float32)
    col = (t % GRID_W).astype(np.float32)
    half = HEAD_DIM // 2
    inv = (np.float32(ROPE_THETA) ** (-np.arange(0, half, 2, dtype=np.float32) / np.float32(half))).astype(np.float32)
    ang_r = row[:, None] * inv[None, :]
    ang_c = col[:, None] * inv[None, :]
    ang = np.concatenate([ang_r, ang_r, ang_c, ang_c], axis=-1).astype(np.float64)
    cos, sin = np.cos(ang), np.sin(ang)
    first = (np.arange(HEAD_DIM) % half) < half // 2
    sin_lo = np.where(first[None, :], -sin, 0.0)
    sin_hi = np.where(first[None, :], 0.0, sin)
    reps = V7X_LANES // HEAD_DIM
    return tuple(jnp.asarray(np.tile(a, (1, reps)), dtype=F32) for a in (cos, sin_lo, sin_hi))


def _identity_rope_tables(n_tokens):
    one = jnp.ones((n_tokens, V7X_LANES), F32)
    zero = jnp.zeros((n_tokens, V7X_LANES), F32)
    return one, zero, zero


def _head_sum_matrix():
    idx = np.arange(V7X_MXU_COLS) // HEAD_DIM
    return jnp.asarray(idx[:, None] == idx[None, :], dtype=BF16)


def kernel(x, c, ctx, c_ctx, norm_g, w_ada, b_ada, w_in, q_norm_g, k_norm_g, conv_w, conv_b, w_attn_br, w_conv_br,
           b_gate, w_out):
    depth = w_ada.shape[0]
    bsz, seq, _ = x.shape
    ctx_len = ctx.shape[1]
    assert bsz + 1 <= MOD_ROWS and seq % ROW_TILE == 0 and ctx_len % HALO == 0 and ctx_len <= ROW_TILE

    cond = jnp.concatenate([c, c_ctx[None, :], jnp.zeros((MOD_ROWS - bsz - 1, D_MODEL), F32)], axis=0)
    mod = _modulation(cond, w_ada, b_ada)
    ctx_row = bsz

    ones = _head_sum_matrix()
    rope_x = _rope_tables(seq)
    rope_c = _identity_rope_tables(ctx_len)
    reps = V7X_MXU_COLS // HEAD_DIM

    for l in range(depth):
        last = l == depth - 1
        mod_l = mod[l].reshape(MOD_ROWS, 1, 3 * D_MODEL)
        w_in_l = w_in[l].astype(BF16)
        ng = norm_g[l].reshape(1, D_MODEL)
        qg = jnp.tile(q_norm_g[l], reps).reshape(1, V7X_MXU_COLS)
        kg = jnp.tile(k_norm_g[l], reps).reshape(1, V7X_MXU_COLS)
        cb = conv_b[l].reshape(1, CONV_WIDTH)
        wcb = w_conv_br[l].astype(BF16)
        wab = w_attn_br[l].astype(BF16)
        wo = w_out[l].astype(BF16)
        shared = dict(conv_w=conv_w[l], conv_b=cb, w_conv_br=wcb, b_gate=b_gate[l])

        qx, ktx, vx, szax, g0x, t1x = _projection(x, mod_l, None, ng, w_in_l, qg, kg, ones, rope_x,
                                                   rope=True, full=True, **shared)
        if last:
            w_kv = w_in_l[:, OFF_K:OFF_K + 2 * KV_WIDTH]
            ktc, vc = _projection(ctx, mod_l, ctx_row, ng, w_kv, None, kg, ones, rope_c,
                                  rope=False, full=False, conv_w=None, conv_b=None, w_conv_br=None, b_gate=None)
        else:
            qc, ktc, vc, szac, g0c, t1c = _projection(ctx, mod_l, ctx_row, ng, w_in_l, qg, kg, ones, rope_c,
                                                       rope=False, full=True, **shared)
        x_new = _attention(qx, [ktc, ktx], [vc, vx], szax, g0x, t1x, x, mod_l, None, wab, wo)
        if not last:
            ctx = _attention(qc, [ktc], [vc], szac, g0c, t1c, ctx, mod_l, ctx_row, wab, wo)
        x = x_new
    return x
```

```python
import functools

import numpy as np
import jax
import jax.numpy as jnp
from jax import lax
from jax.experimental import pallas as pl
from jax.experimental.pallas import tpu as pltpu

D_MODEL = 1024
GRID_W = 64
HEAD_DIM = 64
N_Q_HEADS = 16
N_KV_HEADS = 4
ATTN_WIDTH = N_Q_HEADS * HEAD_DIM
KV_WIDTH = N_KV_HEADS * HEAD_DIM
CONV_WIDTH = D_MODEL
ROPE_THETA = 10000.0
EPS = 1e-6
ATTN_SCALE = HEAD_DIM ** -0.5
SCORE_SCALE = ATTN_SCALE * 1.4426950408889634

OFF_Q = 0
OFF_K = OFF_Q + ATTN_WIDTH
OFF_V = OFF_K + KV_WIDTH
OFF_ZA = OFF_V + KV_WIDTH
OFF_XC = OFF_ZA + ATTN_WIDTH
OFF_BC = OFF_XC + CONV_WIDTH
OFF_CC = OFF_BC + CONV_WIDTH
OFF_ZC = OFF_CC + CONV_WIDTH
OFF_GL = OFF_ZC + CONV_WIDTH
PROJ_WIDTH = OFF_GL + 2 * D_MODEL

V7X_LANES = 128
V7X_MXU_COLS = 256
V7X_BF16_SUBLANES = 16
V7X_VMEM_BYTES = 64 * 1024 * 1024
HALO = V7X_BF16_SUBLANES
ROW_TILE = 256
PV_ROWS = HEAD_DIM + V7X_BF16_SUBLANES
KEY_TILE = V7X_MXU_COLS
SCORE_LOOKAHEAD = 2
MOD_ROWS = 16
MOD_COLS = 512

BF16 = jnp.bfloat16
F32 = jnp.float32


def _dot(a, b):
    return jnp.dot(a, b, preferred_element_type=F32)


def _silu(x):
    return x * jax.nn.sigmoid(x)


def _modulation_kernel(cond_ref, w_ref, b_ref, o_ref):
    a = _silu(cond_ref[...]).astype(BF16)
    o_ref[0] = _dot(a, w_ref[0].astype(BF16)) + b_ref[0]


def _modulation(cond, w_ada, b_ada):
    depth = w_ada.shape[0]
    n_out = w_ada.shape[2]
    return pl.pallas_call(
        _modulation_kernel,
        grid=(depth, n_out // MOD_COLS),
        in_specs=[
            pl.BlockSpec((MOD_ROWS, D_MODEL), lambda l, j: (0, 0)),
            pl.BlockSpec((1, D_MODEL, MOD_COLS), lambda l, j: (l, 0, j)),
            pl.BlockSpec((1, 1, MOD_COLS), lambda l, j: (l, 0, j)),
        ],
        out_specs=pl.BlockSpec((1, MOD_ROWS, MOD_COLS), lambda l, j: (l, 0, j)),
        out_shape=jax.ShapeDtypeStruct((depth, MOD_ROWS, n_out), F32),
        name="modulation",
    )(cond, w_ada, b_ada.reshape(depth, 1, n_out))


def _head_norm_rope(raw, gain, ones_ref, rope_refs, row0, tm, out_scale):
    sumsq = _dot((raw * raw).astype(BF16), ones_ref[...])
    r = lax.rsqrt(sumsq * (1.0 / HEAD_DIM) + EPS)
    a = raw * gain
    if rope_refs is None:
        return a * (r * out_scale) if out_scale != 1.0 else a * r
    cos_ref, sin_lo_ref, sin_hi_ref = rope_refs
    rows = pl.ds(row0, tm)
    cos, sin_lo, sin_hi = cos_ref[rows, :], sin_lo_ref[rows, :], sin_hi_ref[rows, :]
    parts = []
    for m in range(V7X_MXU_COLS // V7X_LANES):
        c = a[:, m * V7X_LANES:(m + 1) * V7X_LANES]
        up = pltpu.roll(c, V7X_LANES - HEAD_DIM // 4, 1)
        down = pltpu.roll(c, HEAD_DIM // 4, 1)
        parts.append(c * cos + up * sin_lo + down * sin_hi)
    rot = jnp.concatenate(parts, axis=1)
    return rot * (r * out_scale) if out_scale != 1.0 else rot * r


def _projection_kernel(*refs, rope, full, tm, n_tiles):
    if full:
        (x_ref, xp_ref, xn_ref, shift_ref, scale_ref, ng_ref, w_ref, qg_ref, kg_ref, ones_ref,
         cos_ref, sin_lo_ref, sin_hi_ref, cw_ref, cb_ref, wcb_ref, bg_ref,
         qt_out, k_out, vt_out, sza_out, g0_out, t1_out, h_scr, u_scr, yz_scr) = refs
    else:
        (x_ref, shift_ref, scale_ref, ng_ref, w_ref, kg_ref, ones_ref,
         cos_ref, sin_lo_ref, sin_hi_ref, k_out, vt_out, h_scr) = refs
    i = pl.program_id(1)
    rope_refs = (cos_ref, sin_lo_ref, sin_hi_ref) if rope else None
    row0 = pl.multiple_of(i * tm, tm)

    mod_gain = ng_ref[...] * (1.0 + scale_ref[0])
    mod_shift = shift_ref[0]

    def modulated(xv):
        ms = jnp.mean(xv * xv, axis=-1, keepdims=True)
        return ((xv * lax.rsqrt(ms + EPS)) * mod_gain + mod_shift).astype(BF16)

    if full:
        h_scr[0:HALO, :] = modulated(xp_ref[0])
        h_scr[HALO:HALO + tm, :] = modulated(x_ref[0])
        h_scr[HALO + tm:, :] = modulated(xn_ref[0])
        h_main = h_scr[HALO:HALO + tm, :]
    else:
        h_scr[...] = modulated(x_ref[0])
        h_main = h_scr[...]

    def w_cols(off, g):
        return w_ref[:, off + g * V7X_MXU_COLS: off + (g + 1) * V7X_MXU_COLS]

    def group(off, g):
        return _dot(h_main, w_cols(off, g))

    k_w = w_cols(OFF_K, 0) if full else w_ref[:, 0:KV_WIDTH]
    v_w = w_cols(OFF_V, 0) if full else w_ref[:, KV_WIDTH:2 * KV_WIDTH]
    k = _head_norm_rope(_dot(h_main, k_w), kg_ref[...], ones_ref, rope_refs, row0, tm, 1.0)
    v = _dot(h_main, v_w)
    for m in range(KV_WIDTH // V7X_LANES):
        lanes = slice(m * V7X_LANES, (m + 1) * V7X_LANES)
        k_pair = k[:, lanes]
        k_out[0, 2 * m] = k_pair[:, :HEAD_DIM].astype(BF16)
        k_out[0, 2 * m + 1] = pltpu.roll(k_pair, HEAD_DIM, 1)[:, :HEAD_DIM].astype(BF16)
        vt_pair = v[:, lanes].T.astype(BF16)
        vt_out[0, 2 * m, 0:HEAD_DIM, :] = vt_pair[:HEAD_DIM]
        vt_out[0, 2 * m + 1, 0:HEAD_DIM, :] = vt_pair[HEAD_DIM:]
    vt_out[0, :, HEAD_DIM:, :] = jnp.ones((N_KV_HEADS, PV_ROWS - HEAD_DIM, tm), BF16)
    if not full:
        return

    n_groups = ATTN_WIDTH // V7X_MXU_COLS
    for g in range(n_groups):
        cols = slice(g * V7X_MXU_COLS, (g + 1) * V7X_MXU_COLS)
        q = _head_norm_rope(group(OFF_Q, g), qg_ref[...], ones_ref, rope_refs, row0, tm, SCORE_SCALE)
        for m in range(V7X_MXU_COLS // V7X_LANES):
            r0 = g * V7X_MXU_COLS + m * V7X_LANES
            qt_out[0, r0:r0 + V7X_LANES, :] = q[:, m * V7X_LANES:(m + 1) * V7X_LANES].T.astype(BF16)
        sza_out[0, :, cols] = _silu(group(OFF_ZA, g)).astype(BF16)

    h_ext = h_scr[...]
    row = lax.broadcasted_iota(jnp.int32, (tm, V7X_MXU_COLS), 0)
    first_row = jnp.logical_and(row == 0, i == 0)
    last_row = jnp.logical_and(row == tm - 1, i == n_tiles - 1)
    for g in range(CONV_WIDTH // V7X_MXU_COLS):
        cols = slice(g * V7X_MXU_COLS, (g + 1) * V7X_MXU_COLS)
        u_scr[...] = _dot(h_ext, w_cols(OFF_CC, g)) * _dot(h_ext, w_cols(OFF_XC, g))
        u_prev = jnp.where(first_row, 0.0, u_scr[HALO - 1:HALO - 1 + tm, :])
        u_next = jnp.where(last_row, 0.0, u_scr[HALO + 1:HALO + 1 + tm, :])
        conv = (u_prev * cw_ref[0:1, cols] + u_scr[HALO:HALO + tm, :] * cw_ref[1:2, cols]
                + u_next * cw_ref[2:3, cols] + cb_ref[:, cols])
        y = group(OFF_BC, g) * conv
        yz_scr[:, cols] = (y * _silu(group(OFF_ZC, g))).astype(BF16)

    yz = yz_scr[...]
    for g in range(D_MODEL // V7X_MXU_COLS):
        cols = slice(g * V7X_MXU_COLS, (g + 1) * V7X_MXU_COLS)
        g0_out[0, :, cols] = jax.nn.sigmoid(group(OFF_GL, g) + bg_ref[0:1, cols]).astype(BF16)
        gate1 = jax.nn.sigmoid(group(OFF_GL + D_MODEL, g) + bg_ref[1:2, cols])
        t1_out[0, :, cols] = gate1 * _dot(yz, wcb_ref[:, cols])


def _const_spec(shape):
    return pl.BlockSpec(shape, lambda b, i: (0,) * len(shape), pipeline_mode=pl.Buffered(1))


def _layer_spec(shape, layer, col_block=0):
    zeros = (0,) * (len(shape) - 1)
    return pl.BlockSpec((None, *shape), lambda b, i: (layer, *zeros, col_block), pipeline_mode=pl.Buffered(1))


def _projection(x, mod, mod_row, layer, p, rope_tabs, *, rope, full):
    bsz, seq, _ = x.shape
    tm = min(ROW_TILE, seq)
    n_tiles = seq // tm
    halo_per_tile = tm // HALO
    n_halo = seq // HALO
    mod_base = layer * MOD_ROWS

    def tile_specs(tile_of):
        def mod_spec(part):
            if mod_row is None:
                return pl.BlockSpec((1, 1, D_MODEL), lambda b, i: (mod_base + tile_of(b, i)[0], 0, part))
            return pl.BlockSpec((1, 1, D_MODEL), lambda b, i: (mod_base + mod_row, 0, part))

        def prev_halo(b, i):
            tb, ti = tile_of(b, i)
            return tb, jnp.maximum(ti * halo_per_tile - 1, 0), 0

        def next_halo(b, i):
            tb, ti = tile_of(b, i)
            return tb, jnp.minimum((ti + 1) * halo_per_tile, n_halo - 1), 0

        return [pl.BlockSpec((1, tm, D_MODEL), lambda b, i: (*tile_of(b, i), 0)),
                pl.BlockSpec((1, HALO, D_MODEL), prev_halo), pl.BlockSpec((1, HALO, D_MODEL), next_halo),
                mod_spec(0), mod_spec(1)]


    tab_spec = _const_spec(rope_tabs[0].shape)
    k_spec = pl.BlockSpec((1, N_KV_HEADS, tm, HEAD_DIM), lambda b, i: (b, 0, i, 0))
    vt_spec = pl.BlockSpec((1, N_KV_HEADS, PV_ROWS, tm), lambda b, i: (b, 0, 0, i))
    k_shape = jax.ShapeDtypeStruct((bsz, N_KV_HEADS, seq, HEAD_DIM), BF16)
    vt_shape = jax.ShapeDtypeStruct((bsz, N_KV_HEADS, PV_ROWS, seq), BF16)
    qt_spec = pl.BlockSpec((1, ATTN_WIDTH, tm), lambda b, i: (b, 0, i))
    row_spec = pl.BlockSpec((1, tm, D_MODEL), lambda b, i: (b, i, 0))
    vmem_limit = V7X_VMEM_BYTES * 7 // 8
    gain_spec = _layer_spec((1, V7X_MXU_COLS), layer)

    kern = functools.partial(_projection_kernel, rope=rope, full=full, tm=tm, n_tiles=n_tiles)
    if full:
        in_specs = [*tile_specs(lambda b, i: (b, i)), _layer_spec((1, D_MODEL), layer),
                    _layer_spec((D_MODEL, PROJ_WIDTH), layer), gain_spec, gain_spec,
                    _const_spec(p["ones"].shape), tab_spec, tab_spec, tab_spec,
                    _layer_spec(p["conv_w"].shape[1:], layer), _layer_spec((1, CONV_WIDTH), layer),
                    _layer_spec((CONV_WIDTH, D_MODEL), layer), _layer_spec(p["b_gate"].shape[1:], layer)]
        args = (x, x, x, mod, mod, p["norm_g"], p["w_in"], p["q_gain"], p["k_gain"], p["ones"], *rope_tabs,
                p["conv_w"], p["conv_b"], p["w_conv_br"], p["b_gate"])
        out_specs = [qt_spec, k_spec, vt_spec, row_spec, row_spec, row_spec]
        out_shape = [jax.ShapeDtypeStruct((bsz, ATTN_WIDTH, seq), BF16), k_shape, vt_shape,
                     jax.ShapeDtypeStruct((bsz, seq, ATTN_WIDTH), BF16),
                     jax.ShapeDtypeStruct((bsz, seq, D_MODEL), BF16),
                     jax.ShapeDtypeStruct((bsz, seq, D_MODEL), F32)]
        scratch = [pltpu.VMEM((tm + 2 * HALO, D_MODEL), BF16),
                   pltpu.VMEM((tm + 2 * HALO, V7X_MXU_COLS), F32),
                   pltpu.VMEM((tm, CONV_WIDTH), BF16)]
    else:
        kv_cols = 2 * KV_WIDTH
        x_spec, _, _, shift_spec, scale_spec = tile_specs(lambda b, i: (b, i))
        in_specs = [x_spec, shift_spec, scale_spec, _layer_spec((1, D_MODEL), layer),
                    _layer_spec((D_MODEL, kv_cols), layer, col_block=OFF_K // kv_cols),
                    gain_spec, _const_spec(p["ones"].shape), tab_spec, tab_spec, tab_spec]
        args = (x, mod, mod, p["norm_g"], p["w_in"], p["k_gain"], p["ones"], *rope_tabs)
        out_specs = [k_spec, vt_spec]
        out_shape = [k_shape, vt_shape]
        scratch = [pltpu.VMEM((tm, D_MODEL), BF16)]
    return pl.pallas_call(
        kern,
        grid=(bsz, n_tiles),
        in_specs=in_specs,
        out_specs=out_specs,
        out_shape=out_shape,
        scratch_shapes=scratch,
        compiler_params=pltpu.CompilerParams(dimension_semantics=("arbitrary", "arbitrary"),
                                             vmem_limit_bytes=vmem_limit),
        name="projection_full" if full else "projection_kv",
    )(*args)


def _attention_kernel(*refs, n_seg):
    slot0_ref, qt_ref = refs[0], refs[1]
    refs = refs[1:]
    k_refs, vt_refs = refs[1:1 + n_seg], refs[1 + n_seg:1 + 2 * n_seg]
    (sza_ref, g0_ref, t1_ref, x_ref, gate_ref, wab_ref, wout_ref, o_ref, attn_scr, st_scr) = refs[1 + 2 * n_seg:]
    slot0 = slot0_ref[0]
    tiles = [(seg, t) for seg in range(n_seg) for t in range(k_refs[seg].shape[2] // KEY_TILE)]
    n_tiles = len(tiles)
    group = N_Q_HEADS // N_KV_HEADS
    sub = 8
    slots = st_scr.shape[0]

    def score_tile(h, t, m8):
        seg, local = tiles[t]
        s = _dot(k_refs[seg][0, h // group, local * KEY_TILE:(local + 1) * KEY_TILE, :],
                 qt_ref[0, h * HEAD_DIM:(h + 1) * HEAD_DIM, :])
        st_scr[slot0 + h % slots, t * KEY_TILE:(t + 1) * KEY_TILE, :] = s.astype(BF16)
        for r in range(KEY_TILE // sub):
            piece = s[r * sub:(r + 1) * sub]
            m8 = piece if m8 is None else jnp.maximum(m8, piece)
        return m8

    def pv_tile(h, t, m_row, acc):
        seg, local = tiles[t]
        p = jnp.exp2(st_scr[slot0 + h % slots, t * KEY_TILE:(t + 1) * KEY_TILE, :] - m_row)
        part = _dot(vt_refs[seg][0, h // group, :, local * KEY_TILE:(local + 1) * KEY_TILE], p)
        return part if acc is None else acc + part

    col_max = {}
    for h in range(min(SCORE_LOOKAHEAD, N_Q_HEADS)):
        m8 = None
        for t in range(n_tiles):
            m8 = score_tile(h, t, m8)
        col_max[h] = jnp.max(m8, axis=0, keepdims=True).astype(BF16)
    halves = []
    for h in range(N_Q_HEADS):
        acc, m8 = None, None
        ahead = h + SCORE_LOOKAHEAD
        for t in range(n_tiles):
            if ahead < N_Q_HEADS:
                m8 = score_tile(ahead, t, m8)
            acc = pv_tile(h, t, col_max[h], acc)
        if ahead < N_Q_HEADS:
            col_max[ahead] = jnp.max(m8, axis=0, keepdims=True).astype(BF16)
        halves.append(acc[:HEAD_DIM] / acc[HEAD_DIM:HEAD_DIM + 1])
        if len(halves) == V7X_LANES // HEAD_DIM:
            j = h // len(halves)
            lanes = slice(j * V7X_LANES, (j + 1) * V7X_LANES)
            attn = jnp.concatenate(halves, axis=0).T
            attn_scr[:, lanes] = (attn * sza_ref[0, :, lanes].astype(F32)).astype(BF16)
            halves = []
    attn_br = _dot(attn_scr[...], wab_ref[...])
    merged = g0_ref[0].astype(F32) * attn_br + t1_ref[0]
    out = _dot(merged.astype(BF16), wout_ref[...])
    o_ref[0] = x_ref[0] + gate_ref[0] * out


def _attention(qt, ks, vts, sza, g0, t1, x, mod, mod_row, layer, w_attn_br, w_out):
    bsz, seq, _ = x.shape
    tq = min(ROW_TILE, seq)
    n_keys = sum(k.shape[2] for k in ks)
    mod_base = layer * MOD_ROWS
    k_specs = [pl.BlockSpec((1, N_KV_HEADS, k.shape[2], HEAD_DIM), lambda b, i: (b, 0, 0, 0)) for k in ks]
    vt_specs = [pl.BlockSpec((1, N_KV_HEADS, PV_ROWS, vt.shape[3]), lambda b, i: (b, 0, 0, 0)) for vt in vts]
    row_spec = pl.BlockSpec((1, tq, D_MODEL), lambda b, i: (b, i, 0))
    if mod_row is None:
        gate_spec = pl.BlockSpec((1, 1, D_MODEL), lambda b, i: (mod_base + b, 0, 2))
    else:
        gate_spec = pl.BlockSpec((1, 1, D_MODEL), lambda b, i: (mod_base + mod_row, 0, 2))
    return pl.pallas_call(
        functools.partial(_attention_kernel, n_seg=len(ks)),
        grid=(bsz, seq // tq),
        in_specs=[pl.BlockSpec(memory_space=pltpu.SMEM),
                  pl.BlockSpec((1, ATTN_WIDTH, tq), lambda b, i: (b, 0, i)), *k_specs, *vt_specs,
                  row_spec, row_spec, row_spec, row_spec, gate_spec,
                  _layer_spec(w_attn_br.shape[1:], layer), _layer_spec(w_out.shape[1:], layer)],
        out_specs=row_spec,
        out_shape=jax.ShapeDtypeStruct(x.shape, F32),
        scratch_shapes=[pltpu.VMEM((tq, ATTN_WIDTH), BF16), pltpu.VMEM((SCORE_LOOKAHEAD + 1, n_keys, tq), BF16)],
        compiler_params=pltpu.CompilerParams(dimension_semantics=("arbitrary", "arbitrary"),
                                             vmem_limit_bytes=V7X_VMEM_BYTES * 3 // 4),
        name="attention_merge",
    )(jnp.zeros((1,), jnp.int32), qt, *ks, *vts, sza, g0, t1, x, mod, w_attn_br, w_out)


def _rope_tables(n_tokens):
    t = np.arange(n_tokens)
    row = (t // GRID_W).astype(np.float32)
    col = (t % GRID_W).astype(np.float32)
    half = HEAD_DIM // 2
    inv = (np.float32(ROPE_THETA) ** (-np.arange(0, half, 2, dtype=np.float32) / np.float32(half))).astype(np.float32)
    ang_r = row[:, None] * inv[None, :]
    ang_c = col[:, None] * inv[None, :]
    ang = np.concatenate([ang_r, ang_r, ang_c, ang_c], axis=-1).astype(np.float64)
    cos, sin = np.cos(ang), np.sin(ang)
    first = (np.arange(HEAD_DIM) % half) < half // 2
    sin_lo = np.where(first[None, :], -sin, 0.0)
    sin_hi = np.where(first[None, :], 0.0, sin)
    reps = V7X_LANES // HEAD_DIM
    return tuple(jnp.asarray(np.tile(a, (1, reps)), dtype=F32) for a in (cos, sin_lo, sin_hi))


def _identity_rope_tables(n_tokens):
    one = jnp.ones((n_tokens, V7X_LANES), F32)
    zero = jnp.zeros((n_tokens, V7X_LANES), F32)
    return one, zero, zero


def _head_sum_matrix():
    idx = np.arange(V7X_MXU_COLS) // HEAD_DIM
    return jnp.asarray(idx[:, None] == idx[None, :], dtype=BF16)


def kernel(x, c, ctx, c_ctx, norm_g, w_ada, b_ada, w_in, q_norm_g, k_norm_g, conv_w, conv_b, w_attn_br, w_conv_br,
           b_gate, w_out):
    depth = w_ada.shape[0]
    bsz, seq, _ = x.shape
    ctx_len = ctx.shape[1]
    assert bsz + 1 <= MOD_ROWS and seq % ROW_TILE == 0 and ctx_len % HALO == 0 and ctx_len <= ROW_TILE

    cond = jnp.concatenate([c, c_ctx[None, :], jnp.zeros((MOD_ROWS - bsz - 1, D_MODEL), F32)], axis=0)
    mod = _modulation(cond, w_ada, b_ada).reshape(depth * MOD_ROWS, 1, 3 * D_MODEL)
    ctx_row = bsz

    rope_x = _rope_tables(seq)
    rope_c = _identity_rope_tables(ctx_len)
    reps = V7X_MXU_COLS // HEAD_DIM
    p = dict(
        norm_g=norm_g.reshape(depth, 1, D_MODEL),
        w_in=w_in.astype(BF16),
        q_gain=jnp.tile(q_norm_g, (1, reps)).reshape(depth, 1, V7X_MXU_COLS),
        k_gain=jnp.tile(k_norm_g, (1, reps)).reshape(depth, 1, V7X_MXU_COLS),
        ones=_head_sum_matrix(),
        conv_w=conv_w,
        conv_b=conv_b.reshape(depth, 1, CONV_WIDTH),
        w_conv_br=w_conv_br.astype(BF16),
        b_gate=b_gate,
    )
    wab = w_attn_br.astype(BF16)
    wo = w_out.astype(BF16)

    for l in range(depth):
        last = l == depth - 1
        qx, ktx, vx, szax, g0x, t1x = _projection(x, mod, None, l, p, rope_x, rope=True, full=True)
        if last:
            ktc, vc = _projection(ctx, mod, ctx_row, l, p, rope_c, rope=False, full=False)
        else:
            qc, ktc, vc, szac, g0c, t1c = _projection(ctx, mod, ctx_row, l, p, rope_c, rope=False, full=True)
        x_new = _attention(qx, [ktc, ktx], [vc, vx], szax, g0x, t1x, x, mod, None, l, wab, wo)
        if not last:
            ctx = _attention(qc, [ktc], [vc], szac, g0c, t1c, ctx, mod, ctx_row, l, wab, wo)
        x = x_new
    return x
```

```python
import functools

import numpy as np
import jax
import jax.numpy as jnp
from jax import lax
from jax.experimental import pallas as pl
from jax.experimental.pallas import tpu as pltpu

D_MODEL = 1024
GRID_W = 64
HEAD_DIM = 64
N_Q_HEADS = 16
N_KV_HEADS = 4
ATTN_WIDTH = N_Q_HEADS * HEAD_DIM
KV_WIDTH = N_KV_HEADS * HEAD_DIM
CONV_WIDTH = D_MODEL
ROPE_THETA = 10000.0
EPS = 1e-6
ATTN_SCALE = HEAD_DIM ** -0.5
SCORE_SCALE = ATTN_SCALE * 1.4426950408889634

OFF_Q = 0
OFF_K = OFF_Q + ATTN_WIDTH
OFF_V = OFF_K + KV_WIDTH
OFF_ZA = OFF_V + KV_WIDTH
OFF_XC = OFF_ZA + ATTN_WIDTH
OFF_BC = OFF_XC + CONV_WIDTH
OFF_CC = OFF_BC + CONV_WIDTH
OFF_ZC = OFF_CC + CONV_WIDTH
OFF_GL = OFF_ZC + CONV_WIDTH
PROJ_WIDTH = OFF_GL + 2 * D_MODEL

V7X_LANES = 128
V7X_MXU_COLS = 256
V7X_BF16_SUBLANES = 16
V7X_VMEM_BYTES = 64 * 1024 * 1024
HALO = V7X_BF16_SUBLANES
ROW_TILE = 256
PV_ROWS = HEAD_DIM + V7X_BF16_SUBLANES
KEY_TILE = V7X_MXU_COLS
SCORE_LOOKAHEAD = 2
MOD_ROWS = 16
MOD_COLS = 512

BF16 = jnp.bfloat16
F32 = jnp.float32


def _dot(a, b):
    return jnp.dot(a, b, preferred_element_type=F32)


def _silu(x):
    return x * jax.nn.sigmoid(x)


def _modulation_kernel(cond_ref, w_ref, b_ref, o_ref):
    a = _silu(cond_ref[...]).astype(BF16)
    o_ref[0] = _dot(a, w_ref[0].astype(BF16)) + b_ref[0]


def _modulation(cond, w_ada, b_ada):
    depth = w_ada.shape[0]
    n_out = w_ada.shape[2]
    return pl.pallas_call(
        _modulation_kernel,
        grid=(depth, n_out // MOD_COLS),
        in_specs=[
            pl.BlockSpec((MOD_ROWS, D_MODEL), lambda l, j: (0, 0)),
            pl.BlockSpec((1, D_MODEL, MOD_COLS), lambda l, j: (l, 0, j)),
            pl.BlockSpec((1, 1, MOD_COLS), lambda l, j: (l, 0, j)),
        ],
        out_specs=pl.BlockSpec((1, MOD_ROWS, MOD_COLS), lambda l, j: (l, 0, j)),
        out_shape=jax.ShapeDtypeStruct((depth, MOD_ROWS, n_out), F32),
        name="modulation",
    )(cond, w_ada, b_ada.reshape(depth, 1, n_out))


def _head_norm_rope(raw, gain, ones_ref, rope_refs, row0, tm, out_scale):
    sumsq = _dot((raw * raw).astype(BF16), ones_ref[...])
    r = lax.rsqrt(sumsq * (1.0 / HEAD_DIM) + EPS)
    a = raw * gain
    if rope_refs is None:
        return a * (r * out_scale) if out_scale != 1.0 else a * r
    cos_ref, sin_lo_ref, sin_hi_ref = rope_refs
    rows = pl.ds(row0, tm)
    cos, sin_lo, sin_hi = cos_ref[rows, :], sin_lo_ref[rows, :], sin_hi_ref[rows, :]
    parts = []
    for m in range(V7X_MXU_COLS // V7X_LANES):
        c = a[:, m * V7X_LANES:(m + 1) * V7X_LANES]
        up = pltpu.roll(c, V7X_LANES - HEAD_DIM // 4, 1)
        down = pltpu.roll(c, HEAD_DIM // 4, 1)
        parts.append(c * cos + up * sin_lo + down * sin_hi)
    rot = jnp.concatenate(parts, axis=1)
    return rot * (r * out_scale) if out_scale != 1.0 else rot * r


def _projection_kernel(*refs, rope, full, tm, n_tiles):
    if full:
        (x_ref, xp_ref, xn_ref, shift_ref, scale_ref, ng_ref, w_ref, qg_ref, kg_ref, ones_ref,
         cos_ref, sin_lo_ref, sin_hi_ref, cw_ref, cb_ref, wcb_ref, bg_ref,
         qt_out, k_out, vt_out, sza_out, g0_out, t1_out, h_scr, u_scr, yz_scr) = refs
    else:
        (x_ref, shift_ref, scale_ref, ng_ref, w_ref, kg_ref, ones_ref,
         cos_ref, sin_lo_ref, sin_hi_ref, k_out, vt_out, h_scr) = refs
    i = pl.program_id(1)
    rope_refs = (cos_ref, sin_lo_ref, sin_hi_ref) if rope else None
    row0 = pl.multiple_of(i * tm, tm)

    mod_gain = ng_ref[...] * (1.0 + scale_ref[0])
    mod_shift = shift_ref[0]

    def modulated(xv):
        ms = jnp.mean(xv * xv, axis=-1, keepdims=True)
        return ((xv * lax.rsqrt(ms + EPS)) * mod_gain + mod_shift).astype(BF16)

    if full:
        h_scr[0:HALO, :] = modulated(xp_ref[0])
        h_scr[HALO:HALO + tm, :] = modulated(x_ref[0])
        h_scr[HALO + tm:, :] = modulated(xn_ref[0])
        h_main = h_scr[HALO:HALO + tm, :]
    else:
        h_scr[...] = modulated(x_ref[0])
        h_main = h_scr[...]

    def w_cols(off, g):
        return w_ref[:, off + g * V7X_MXU_COLS: off + (g + 1) * V7X_MXU_COLS]

    def group(off, g):
        return _dot(h_main, w_cols(off, g))

    k_w = w_cols(OFF_K, 0) if full else w_ref[:, 0:KV_WIDTH]
    v_w = w_cols(OFF_V, 0) if full else w_ref[:, KV_WIDTH:2 * KV_WIDTH]
    k = _head_norm_rope(_dot(h_main, k_w), kg_ref[...], ones_ref, rope_refs, row0, tm, 1.0)
    v = _dot(h_main, v_w)
    for m in range(KV_WIDTH // V7X_LANES):
        lanes = slice(m * V7X_LANES, (m + 1) * V7X_LANES)
        k_pair = k[:, lanes]
        k_out[0, 2 * m] = k_pair[:, :HEAD_DIM].astype(BF16)
        k_out[0, 2 * m + 1] = pltpu.roll(k_pair, HEAD_DIM, 1)[:, :HEAD_DIM].astype(BF16)
        vt_pair = v[:, lanes].T.astype(BF16)
        vt_out[0, 2 * m, 0:HEAD_DIM, :] = vt_pair[:HEAD_DIM]
        vt_out[0, 2 * m + 1, 0:HEAD_DIM, :] = vt_pair[HEAD_DIM:]
    vt_out[0, :, HEAD_DIM:, :] = jnp.ones((N_KV_HEADS, PV_ROWS - HEAD_DIM, tm), BF16)
    if not full:
        return

    n_groups = ATTN_WIDTH // V7X_MXU_COLS
    for g in range(n_groups):
        cols = slice(g * V7X_MXU_COLS, (g + 1) * V7X_MXU_COLS)
        q = _head_norm_rope(group(OFF_Q, g), qg_ref[...], ones_ref, rope_refs, row0, tm, SCORE_SCALE)
        for m in range(V7X_MXU_COLS // V7X_LANES):
            r0 = g * V7X_MXU_COLS + m * V7X_LANES
            qt_out[0, r0:r0 + V7X_LANES, :] = q[:, m * V7X_LANES:(m + 1) * V7X_LANES].T.astype(BF16)
        sza_out[0, :, cols] = _silu(group(OFF_ZA, g)).astype(BF16)

    h_ext = h_scr[...]
    row = lax.broadcasted_iota(jnp.int32, (tm, V7X_MXU_COLS), 0)
    first_row = jnp.logical_and(row == 0, i == 0)
    last_row = jnp.logical_and(row == tm - 1, i == n_tiles - 1)
    for g in range(CONV_WIDTH // V7X_MXU_COLS):
        cols = slice(g * V7X_MXU_COLS, (g + 1) * V7X_MXU_COLS)
        u_scr[...] = _dot(h_ext, w_cols(OFF_CC, g)) * _dot(h_ext, w_cols(OFF_XC, g))
        u_prev = jnp.where(first_row, 0.0, u_scr[HALO - 1:HALO - 1 + tm, :])
        u_next = jnp.where(last_row, 0.0, u_scr[HALO + 1:HALO + 1 + tm, :])
        conv = (u_prev * cw_ref[0:1, cols] + u_scr[HALO:HALO + tm, :] * cw_ref[1:2, cols]
                + u_next * cw_ref[2:3, cols] + cb_ref[:, cols])
        y = group(OFF_BC, g) * conv
        yz_scr[:, cols] = (y * _silu(group(OFF_ZC, g))).astype(BF16)

    yz = yz_scr[...]
    for g in range(D_MODEL // V7X_MXU_COLS):
        cols = slice(g * V7X_MXU_COLS, (g + 1) * V7X_MXU_COLS)
        g0_out[0, :, cols] = jax.nn.sigmoid(group(OFF_GL, g) + bg_ref[0:1, cols]).astype(BF16)
        gate1 = jax.nn.sigmoid(group(OFF_GL + D_MODEL, g) + bg_ref[1:2, cols])
        t1_out[0, :, cols] = gate1 * _dot(yz, wcb_ref[:, cols])


def _const_spec(shape):
    return pl.BlockSpec(shape, lambda b, i: (0,) * len(shape), pipeline_mode=pl.Buffered(1))


def _layer_spec(shape, layer, col_block=0):
    zeros = (0,) * (len(shape) - 1)
    return pl.BlockSpec((None, *shape), lambda b, i: (layer, *zeros, col_block), pipeline_mode=pl.Buffered(1))


def _projection(x, mod, mod_row, layer, p, rope_tabs, *, rope, full):
    bsz, seq, _ = x.shape
    tm = min(2 * ROW_TILE, seq)
    n_tiles = seq // tm
    halo_per_tile = tm // HALO
    n_halo = seq // HALO
    mod_base = layer * MOD_ROWS

    def tile_specs(tile_of):
        def mod_spec(part):
            if mod_row is None:
                return pl.BlockSpec((1, 1, D_MODEL), lambda b, i: (mod_base + tile_of(b, i)[0], 0, part))
            return pl.BlockSpec((1, 1, D_MODEL), lambda b, i: (mod_base + mod_row, 0, part))

        def prev_halo(b, i):
            tb, ti = tile_of(b, i)
            return tb, jnp.maximum(ti * halo_per_tile - 1, 0), 0

        def next_halo(b, i):
            tb, ti = tile_of(b, i)
            return tb, jnp.minimum((ti + 1) * halo_per_tile, n_halo - 1), 0

        return [pl.BlockSpec((1, tm, D_MODEL), lambda b, i: (*tile_of(b, i), 0)),
                pl.BlockSpec((1, HALO, D_MODEL), prev_halo), pl.BlockSpec((1, HALO, D_MODEL), next_halo),
                mod_spec(0), mod_spec(1)]


    tab_spec = _const_spec(rope_tabs[0].shape)
    k_spec = pl.BlockSpec((1, N_KV_HEADS, tm, HEAD_DIM), lambda b, i: (b, 0, i, 0))
    vt_spec = pl.BlockSpec((1, N_KV_HEADS, PV_ROWS, tm), lambda b, i: (b, 0, 0, i))
    k_shape = jax.ShapeDtypeStruct((bsz, N_KV_HEADS, seq, HEAD_DIM), BF16)
    vt_shape = jax.ShapeDtypeStruct((bsz, N_KV_HEADS, PV_ROWS, seq), BF16)
    qt_spec = pl.BlockSpec((1, ATTN_WIDTH, tm), lambda b, i: (b, 0, i))
    row_spec = pl.BlockSpec((1, tm, D_MODEL), lambda b, i: (b, i, 0))
    vmem_limit = V7X_VMEM_BYTES * 7 // 8
    gain_spec = _layer_spec((1, V7X_MXU_COLS), layer)

    kern = functools.partial(_projection_kernel, rope=rope, full=full, tm=tm, n_tiles=n_tiles)
    if full:
        in_specs = [*tile_specs(lambda b, i: (b, i)), _layer_spec((1, D_MODEL), layer),
                    _layer_spec((D_MODEL, PROJ_WIDTH), layer), gain_spec, gain_spec,
                    _const_spec(p["ones"].shape), tab_spec, tab_spec, tab_spec,
                    _layer_spec(p["conv_w"].shape[1:], layer), _layer_spec((1, CONV_WIDTH), layer),
                    _layer_spec((CONV_WIDTH, D_MODEL), layer), _layer_spec(p["b_gate"].shape[1:], layer)]
        args = (x, x, x, mod, mod, p["norm_g"], p["w_in"], p["q_gain"], p["k_gain"], p["ones"], *rope_tabs,
                p["conv_w"], p["conv_b"], p["w_conv_br"], p["b_gate"])
        out_specs = [qt_spec, k_spec, vt_spec, row_spec, row_spec, row_spec]
        out_shape = [jax.ShapeDtypeStruct((bsz, ATTN_WIDTH, seq), BF16), k_shape, vt_shape,
                     jax.ShapeDtypeStruct((bsz, seq, ATTN_WIDTH), BF16),
                     jax.ShapeDtypeStruct((bsz, seq, D_MODEL), BF16),
                     jax.ShapeDtypeStruct((bsz, seq, D_MODEL), F32)]
        scratch = [pltpu.VMEM((tm + 2 * HALO, D_MODEL), BF16),
                   pltpu.VMEM((tm + 2 * HALO, V7X_MXU_COLS), F32),
                   pltpu.VMEM((tm, CONV_WIDTH), BF16)]
    else:
        kv_cols = 2 * KV_WIDTH
        x_spec, _, _, shift_spec, scale_spec = tile_specs(lambda b, i: (b, i))
        in_specs = [x_spec, shift_spec, scale_spec, _layer_spec((1, D_MODEL), layer),
                    _layer_spec((D_MODEL, kv_cols), layer, col_block=OFF_K // kv_cols),
                    gain_spec, _const_spec(p["ones"].shape), tab_spec, tab_spec, tab_spec]
        args = (x, mod, mod, p["norm_g"], p["w_in"], p["k_gain"], p["ones"], *rope_tabs)
        out_specs = [k_spec, vt_spec]
        out_shape = [k_shape, vt_shape]
        scratch = [pltpu.VMEM((tm, D_MODEL), BF16)]
    return pl.pallas_call(
        kern,
        grid=(bsz, n_tiles),
        in_specs=in_specs,
        out_specs=out_specs,
        out_shape=out_shape,
        scratch_shapes=scratch,
        compiler_params=pltpu.CompilerParams(dimension_semantics=("arbitrary", "arbitrary"),
                                             vmem_limit_bytes=vmem_limit),
        name="projection_full" if full else "projection_kv",
    )(*args)


def _attention_kernel(*refs, n_seg):
    slot0_ref, qt_ref = refs[0], refs[1]
    refs = refs[1:]
    k_refs, vt_refs = refs[1:1 + n_seg], refs[1 + n_seg:1 + 2 * n_seg]
    (sza_ref, g0_ref, t1_ref, x_ref, gate_ref, wab_ref, wout_ref, o_ref, attn_scr, st_scr) = refs[1 + 2 * n_seg:]
    slot0 = slot0_ref[0]
    tiles = [(seg, t) for seg in range(n_seg) for t in range(k_refs[seg].shape[2] // KEY_TILE)]
    n_tiles = len(tiles)
    group = N_Q_HEADS // N_KV_HEADS
    sub = 8
    slots = st_scr.shape[0]

    def score_tile(h, t, m8):
        seg, local = tiles[t]
        s = _dot(k_refs[seg][0, h // group, local * KEY_TILE:(local + 1) * KEY_TILE, :],
                 qt_ref[0, h * HEAD_DIM:(h + 1) * HEAD_DIM, :])
        st_scr[slot0 + h % slots, t * KEY_TILE:(t + 1) * KEY_TILE, :] = s.astype(BF16)
        for r in range(KEY_TILE // sub):
            piece = s[r * sub:(r + 1) * sub]
            m8 = piece if m8 is None else jnp.maximum(m8, piece)
        return m8

    def pv_tile(h, t, m_row, acc):
        seg, local = tiles[t]
        p = jnp.exp2(st_scr[slot0 + h % slots, t * KEY_TILE:(t + 1) * KEY_TILE, :] - m_row)
        part = _dot(vt_refs[seg][0, h // group, :, local * KEY_TILE:(local + 1) * KEY_TILE], p)
        return part if acc is None else acc + part

    col_max = {}
    for h in range(min(SCORE_LOOKAHEAD, N_Q_HEADS)):
        m8 = None
        for t in range(n_tiles):
            m8 = score_tile(h, t, m8)
        col_max[h] = jnp.max(m8, axis=0, keepdims=True).astype(BF16)
    halves = []
    for h in range(N_Q_HEADS):
        acc, m8 = None, None
        ahead = h + SCORE_LOOKAHEAD
        for t in range(n_tiles):
            if ahead < N_Q_HEADS:
                m8 = score_tile(ahead, t, m8)
            acc = pv_tile(h, t, col_max[h], acc)
        if ahead < N_Q_HEADS:
            col_max[ahead] = jnp.max(m8, axis=0, keepdims=True).astype(BF16)
        halves.append(acc[:HEAD_DIM] / acc[HEAD_DIM:HEAD_DIM + 1])
        if len(halves) == V7X_LANES // HEAD_DIM:
            j = h // len(halves)
            lanes = slice(j * V7X_LANES, (j + 1) * V7X_LANES)
            attn = jnp.concatenate(halves, axis=0).T
            attn_scr[:, lanes] = (attn * sza_ref[0, :, lanes].astype(F32)).astype(BF16)
            halves = []
    attn_br = _dot(attn_scr[...], wab_ref[...])
    merged = g0_ref[0].astype(F32) * attn_br + t1_ref[0]
    out = _dot(merged.astype(BF16), wout_ref[...])
    o_ref[0] = x_ref[0] + gate_ref[0] * out


def _attention(qt, ks, vts, sza, g0, t1, x, mod, mod_row, layer, w_attn_br, w_out):
    bsz, seq, _ = x.shape
    tq = min(ROW_TILE, seq)
    n_keys = sum(k.shape[2] for k in ks)
    mod_base = layer * MOD_ROWS
    k_specs = [pl.BlockSpec((1, N_KV_HEADS, k.shape[2], HEAD_DIM), lambda b, i: (b, 0, 0, 0)) for k in ks]
    vt_specs = [pl.BlockSpec((1, N_KV_HEADS, PV_ROWS, vt.shape[3]), lambda b, i: (b, 0, 0, 0)) for vt in vts]
    row_spec = pl.BlockSpec((1, tq, D_MODEL), lambda b, i: (b, i, 0))
    if mod_row is None:
        gate_spec = pl.BlockSpec((1, 1, D_MODEL), lambda b, i: (mod_base + b, 0, 2))
    else:
        gate_spec = pl.BlockSpec((1, 1, D_MODEL), lambda b, i: (mod_base + mod_row, 0, 2))
    return pl.pallas_call(
        functools.partial(_attention_kernel, n_seg=len(ks)),
        grid=(bsz, seq // tq),
        in_specs=[pl.BlockSpec(memory_space=pltpu.SMEM),
                  pl.BlockSpec((1, ATTN_WIDTH, tq), lambda b, i: (b, 0, i)), *k_specs, *vt_specs,
                  row_spec, row_spec, row_spec, row_spec, gate_spec,
                  _layer_spec(w_attn_br.shape[1:], layer), _layer_spec(w_out.shape[1:], layer)],
        out_specs=row_spec,
        out_shape=jax.ShapeDtypeStruct(x.shape, F32),
        scratch_shapes=[pltpu.VMEM((tq, ATTN_WIDTH), BF16), pltpu.VMEM((SCORE_LOOKAHEAD + 1, n_keys, tq), BF16)],
        compiler_params=pltpu.CompilerParams(dimension_semantics=("arbitrary", "arbitrary"),
                                             vmem_limit_bytes=V7X_VMEM_BYTES * 3 // 4),
        name="attention_merge",
    )(jnp.zeros((1,), jnp.int32), qt, *ks, *vts, sza, g0, t1, x, mod, w_attn_br, w_out)


def _rope_tables(n_tokens):
    t = np.arange(n_tokens)
    row = (t // GRID_W).astype(np.float32)
    col = (t % GRID_W).astype(np.float32)
    half = HEAD_DIM // 2
    inv = (np.float32(ROPE_THETA) ** (-np.arange(0, half, 2, dtype=np.float32) / np.float32(half))).astype(np.float32)
    ang_r = row[:, None] * inv[None, :]
    ang_c = col[:, None] * inv[None, :]
    ang = np.concatenate([ang_r, ang_r, ang_c, ang_c], axis=-1).astype(np.float64)
    cos, sin = np.cos(ang), np.sin(ang)
    first = (np.arange(HEAD_DIM) % half) < half // 2
    sin_lo = np.where(first[None, :], -sin, 0.0)
    sin_hi = np.where(first[None, :], 0.0, sin)
    reps = V7X_LANES // HEAD_DIM
    return tuple(jnp.asarray(np.tile(a, (1, reps)), dtype=F32) for a in (cos, sin_lo, sin_hi))


def _identity_rope_tables(n_tokens):
    one = jnp.ones((n_tokens, V7X_LANES), F32)
    zero = jnp.zeros((n_tokens, V7X_LANES), F32)
    return one, zero, zero


def _head_sum_matrix():
    idx = np.arange(V7X_MXU_COLS) // HEAD_DIM
    return jnp.asarray(idx[:, None] == idx[None, :], dtype=BF16)


def kernel(x, c, ctx, c_ctx, norm_g, w_ada, b_ada, w_in, q_norm_g, k_norm_g, conv_w, conv_b, w_attn_br, w_conv_br,
           b_gate, w_out):
    depth = w_ada.shape[0]
    bsz, seq, _ = x.shape
    ctx_len = ctx.shape[1]
    assert bsz + 1 <= MOD_ROWS and seq % ROW_TILE == 0 and ctx_len % HALO == 0 and ctx_len <= ROW_TILE

    cond = jnp.concatenate([c, c_ctx[None, :], jnp.zeros((MOD_ROWS - bsz - 1, D_MODEL), F32)], axis=0)
    mod = _modulation(cond, w_ada, b_ada).reshape(depth * MOD_ROWS, 1, 3 * D_MODEL)
    ctx_row = bsz

    rope_x = _rope_tables(seq)
    rope_c = _identity_rope_tables(ctx_len)
    reps = V7X_MXU_COLS // HEAD_DIM
    p = dict(
        norm_g=norm_g.reshape(depth, 1, D_MODEL),
        w_in=w_in.astype(BF16),
        q_gain=jnp.tile(q_norm_g, (1, reps)).reshape(depth, 1, V7X_MXU_COLS),
        k_gain=jnp.tile(k_norm_g, (1, reps)).reshape(depth, 1, V7X_MXU_COLS),
        ones=_head_sum_matrix(),
        conv_w=conv_w,
        conv_b=conv_b.reshape(depth, 1, CONV_WIDTH),
        w_conv_br=w_conv_br.astype(BF16),
        b_gate=b_gate,
    )
    wab = w_attn_br.astype(BF16)
    wo = w_out.astype(BF16)

    for l in range(depth):
        last = l == depth - 1
        qx, ktx, vx, szax, g0x, t1x = _projection(x, mod, None, l, p, rope_x, rope=True, full=True)
        if last:
            ktc, vc = _projection(ctx, mod, ctx_row, l, p, rope_c, rope=False, full=False)
        else:
            qc, ktc, vc, szac, g0c, t1c = _projection(ctx, mod, ctx_row, l, p, rope_c, rope=False, full=True)
        x_new = _attention(qx, [ktc, ktx], [vc, vx], szax, g0x, t1x, x, mod, None, l, wab, wo)
        if not last:
            ctx = _attention(qc, [ktc], [vc], szac, g0c, t1c, ctx, mod, ctx_row, l, wab, wo)
        x = x_new
    return x
```

```python
import functools

import numpy as np
import jax
import jax.numpy as jnp
from jax import lax
from jax.experimental import pallas as pl
from jax.experimental.pallas import tpu as pltpu

D_MODEL = 1024
GRID_W = 64
HEAD_DIM = 64
N_Q_HEADS = 16
N_KV_HEADS = 4
ATTN_WIDTH = N_Q_HEADS * HEAD_DIM
KV_WIDTH = N_KV_HEADS * HEAD_DIM
CONV_WIDTH = D_MODEL
ROPE_THETA = 10000.0
EPS = 1e-6
ATTN_SCALE = HEAD_DIM ** -0.5
SCORE_SCALE = ATTN_SCALE * 1.4426950408889634

OFF_Q = 0
OFF_K = OFF_Q + ATTN_WIDTH
OFF_V = OFF_K + KV_WIDTH
OFF_ZA = OFF_V + KV_WIDTH
OFF_XC = OFF_ZA + ATTN_WIDTH
OFF_BC = OFF_XC + CONV_WIDTH
OFF_CC = OFF_BC + CONV_WIDTH
OFF_ZC = OFF_CC + CONV_WIDTH
OFF_GL = OFF_ZC + CONV_WIDTH
PROJ_WIDTH = OFF_GL + 2 * D_MODEL

V7X_LANES = 128
V7X_MXU_COLS = 256
V7X_BF16_SUBLANES = 16
V7X_VMEM_BYTES = 64 * 1024 * 1024
HALO = V7X_BF16_SUBLANES
ROW_TILE = 256
PV_ROWS = HEAD_DIM + V7X_BF16_SUBLANES
KEY_TILE = V7X_MXU_COLS
SCORE_LOOKAHEAD = 2
ATTN_SUBTILES = 2
MOD_ROWS = 16
MOD_COLS = 512

BF16 = jnp.bfloat16
F32 = jnp.float32


def _dot(a, b):
    return jnp.dot(a, b, preferred_element_type=F32)


def _silu(x):
    return x * jax.nn.sigmoid(x)


def _modulation_kernel(cond_ref, w_ref, b_ref, o_ref):
    a = _silu(cond_ref[...]).astype(BF16)
    o_ref[0] = _dot(a, w_ref[0].astype(BF16)) + b_ref[0]


def _modulation(cond, w_ada, b_ada):
    depth = w_ada.shape[0]
    n_out = w_ada.shape[2]
    return pl.pallas_call(
        _modulation_kernel,
        grid=(depth, n_out // MOD_COLS),
        in_specs=[
            pl.BlockSpec((MOD_ROWS, D_MODEL), lambda l, j: (0, 0)),
            pl.BlockSpec((1, D_MODEL, MOD_COLS), lambda l, j: (l, 0, j)),
            pl.BlockSpec((1, 1, MOD_COLS), lambda l, j: (l, 0, j)),
        ],
        out_specs=pl.BlockSpec((1, MOD_ROWS, MOD_COLS), lambda l, j: (l, 0, j)),
        out_shape=jax.ShapeDtypeStruct((depth, MOD_ROWS, n_out), F32),
        name="modulation",
    )(cond, w_ada, b_ada.reshape(depth, 1, n_out))


def _head_norm_rope(raw, gain, ones_ref, rope_refs, row0, tm, out_scale):
    sumsq = _dot((raw * raw).astype(BF16), ones_ref[...])
    r = lax.rsqrt(sumsq * (1.0 / HEAD_DIM) + EPS)
    a = raw * gain
    if rope_refs is None:
        return a * (r * out_scale) if out_scale != 1.0 else a * r
    cos_ref, sin_lo_ref, sin_hi_ref = rope_refs
    rows = pl.ds(row0, tm)
    cos, sin_lo, sin_hi = cos_ref[rows, :], sin_lo_ref[rows, :], sin_hi_ref[rows, :]
    parts = []
    for m in range(V7X_MXU_COLS // V7X_LANES):
        c = a[:, m * V7X_LANES:(m + 1) * V7X_LANES]
        up = pltpu.roll(c, V7X_LANES - HEAD_DIM // 4, 1)
        down = pltpu.roll(c, HEAD_DIM // 4, 1)
        parts.append(c * cos + up * sin_lo + down * sin_hi)
    rot = jnp.concatenate(parts, axis=1)
    return rot * (r * out_scale) if out_scale != 1.0 else rot * r


def _projection_kernel(*refs, rope, full, tm, n_tiles):
    if full:
        (x_ref, xp_ref, xn_ref, shift_ref, scale_ref, ng_ref, w_ref, qg_ref, kg_ref, ones_ref,
         cos_ref, sin_lo_ref, sin_hi_ref, cw_ref, cb_ref, wcb_ref, bg_ref,
         qt_out, k_out, vt_out, sza_out, g0_out, t1_out, h_scr, u_scr, yz_scr) = refs
    else:
        (x_ref, shift_ref, scale_ref, ng_ref, w_ref, kg_ref, ones_ref,
         cos_ref, sin_lo_ref, sin_hi_ref, k_out, vt_out, h_scr) = refs
    i = pl.program_id(1)
    rope_refs = (cos_ref, sin_lo_ref, sin_hi_ref) if rope else None
    row0 = pl.multiple_of(i * tm, tm)

    mod_gain = ng_ref[...] * (1.0 + scale_ref[0])
    mod_shift = shift_ref[0]

    def modulated(xv):
        ms = jnp.mean(xv * xv, axis=-1, keepdims=True)
        return ((xv * lax.rsqrt(ms + EPS)) * mod_gain + mod_shift).astype(BF16)

    if full:
        h_scr[0:HALO, :] = modulated(xp_ref[0])
        h_scr[HALO:HALO + tm, :] = modulated(x_ref[0])
        h_scr[HALO + tm:, :] = modulated(xn_ref[0])
        h_main = h_scr[HALO:HALO + tm, :]
    else:
        h_scr[...] = modulated(x_ref[0])
        h_main = h_scr[...]

    def w_cols(off, g):
        return w_ref[:, off + g * V7X_MXU_COLS: off + (g + 1) * V7X_MXU_COLS]

    def group(off, g):
        return _dot(h_main, w_cols(off, g))

    k_w = w_cols(OFF_K, 0) if full else w_ref[:, 0:KV_WIDTH]
    v_w = w_cols(OFF_V, 0) if full else w_ref[:, KV_WIDTH:2 * KV_WIDTH]
    k = _head_norm_rope(_dot(h_main, k_w), kg_ref[...], ones_ref, rope_refs, row0, tm, 1.0)
    v = _dot(h_main, v_w)
    for m in range(KV_WIDTH // V7X_LANES):
        lanes = slice(m * V7X_LANES, (m + 1) * V7X_LANES)
        k_pair = k[:, lanes]
        k_out[0, 2 * m] = k_pair[:, :HEAD_DIM].astype(BF16)
        k_out[0, 2 * m + 1] = pltpu.roll(k_pair, HEAD_DIM, 1)[:, :HEAD_DIM].astype(BF16)
        vt_pair = v[:, lanes].T.astype(BF16)
        vt_out[0, 2 * m, 0:HEAD_DIM, :] = vt_pair[:HEAD_DIM]
        vt_out[0, 2 * m + 1, 0:HEAD_DIM, :] = vt_pair[HEAD_DIM:]
    vt_out[0, :, HEAD_DIM:, :] = jnp.ones((N_KV_HEADS, PV_ROWS - HEAD_DIM, tm), BF16)
    if not full:
        return

    n_groups = ATTN_WIDTH // V7X_MXU_COLS
    for g in range(n_groups):
        cols = slice(g * V7X_MXU_COLS, (g + 1) * V7X_MXU_COLS)
        q = _head_norm_rope(group(OFF_Q, g), qg_ref[...], ones_ref, rope_refs, row0, tm, SCORE_SCALE)
        for m in range(V7X_MXU_COLS // V7X_LANES):
            r0 = g * V7X_MXU_COLS + m * V7X_LANES
            qt_out[0, r0:r0 + V7X_LANES, :] = q[:, m * V7X_LANES:(m + 1) * V7X_LANES].T.astype(BF16)
        sza_out[0, :, cols] = _silu(group(OFF_ZA, g)).astype(BF16)

    h_ext = h_scr[...]
    row = lax.broadcasted_iota(jnp.int32, (tm, V7X_MXU_COLS), 0)
    first_row = jnp.logical_and(row == 0, i == 0)
    last_row = jnp.logical_and(row == tm - 1, i == n_tiles - 1)
    for g in range(CONV_WIDTH // V7X_MXU_COLS):
        cols = slice(g * V7X_MXU_COLS, (g + 1) * V7X_MXU_COLS)
        u_scr[...] = _dot(h_ext, w_cols(OFF_CC, g)) * _dot(h_ext, w_cols(OFF_XC, g))
        u_prev = jnp.where(first_row, 0.0, u_scr[HALO - 1:HALO - 1 + tm, :])
        u_next = jnp.where(last_row, 0.0, u_scr[HALO + 1:HALO + 1 + tm, :])
        conv = (u_prev * cw_ref[0:1, cols] + u_scr[HALO:HALO + tm, :] * cw_ref[1:2, cols]
                + u_next * cw_ref[2:3, cols] + cb_ref[:, cols])
        y = group(OFF_BC, g) * conv
        yz_scr[:, cols] = (y * _silu(group(OFF_ZC, g))).astype(BF16)

    yz = yz_scr[...]
    for g in range(D_MODEL // V7X_MXU_COLS):
        cols = slice(g * V7X_MXU_COLS, (g + 1) * V7X_MXU_COLS)
        g0_out[0, :, cols] = jax.nn.sigmoid(group(OFF_GL, g) + bg_ref[0:1, cols]).astype(BF16)
        gate1 = jax.nn.sigmoid(group(OFF_GL + D_MODEL, g) + bg_ref[1:2, cols])
        t1_out[0, :, cols] = gate1 * _dot(yz, wcb_ref[:, cols])


def _const_spec(shape):
    return pl.BlockSpec(shape, lambda b, i: (0,) * len(shape), pipeline_mode=pl.Buffered(1))


def _layer_spec(shape, layer, col_block=0):
    zeros = (0,) * (len(shape) - 1)
    return pl.BlockSpec((None, *shape), lambda b, i: (layer, *zeros, col_block), pipeline_mode=pl.Buffered(1))


def _projection(x, mod, mod_row, layer, p, rope_tabs, *, rope, full):
    bsz, seq, _ = x.shape
    tm = min(2 * ROW_TILE, seq)
    n_tiles = seq // tm
    halo_per_tile = tm // HALO
    n_halo = seq // HALO
    mod_base = layer * MOD_ROWS

    def tile_specs(tile_of):
        def mod_spec(part):
            if mod_row is None:
                return pl.BlockSpec((1, 1, D_MODEL), lambda b, i: (mod_base + tile_of(b, i)[0], 0, part))
            return pl.BlockSpec((1, 1, D_MODEL), lambda b, i: (mod_base + mod_row, 0, part))

        def prev_halo(b, i):
            tb, ti = tile_of(b, i)
            return tb, jnp.maximum(ti * halo_per_tile - 1, 0), 0

        def next_halo(b, i):
            tb, ti = tile_of(b, i)
            return tb, jnp.minimum((ti + 1) * halo_per_tile, n_halo - 1), 0

        return [pl.BlockSpec((1, tm, D_MODEL), lambda b, i: (*tile_of(b, i), 0)),
                pl.BlockSpec((1, HALO, D_MODEL), prev_halo), pl.BlockSpec((1, HALO, D_MODEL), next_halo),
                mod_spec(0), mod_spec(1)]


    tab_spec = _const_spec(rope_tabs[0].shape)
    k_spec = pl.BlockSpec((1, N_KV_HEADS, tm, HEAD_DIM), lambda b, i: (b, 0, i, 0))
    vt_spec = pl.BlockSpec((1, N_KV_HEADS, PV_ROWS, tm), lambda b, i: (b, 0, 0, i))
    k_shape = jax.ShapeDtypeStruct((bsz, N_KV_HEADS, seq, HEAD_DIM), BF16)
    vt_shape = jax.ShapeDtypeStruct((bsz, N_KV_HEADS, PV_ROWS, seq), BF16)
    qt_spec = pl.BlockSpec((1, ATTN_WIDTH, tm), lambda b, i: (b, 0, i))
    row_spec = pl.BlockSpec((1, tm, D_MODEL), lambda b, i: (b, i, 0))
    vmem_limit = V7X_VMEM_BYTES * 7 // 8
    gain_spec = _layer_spec((1, V7X_MXU_COLS), layer)

    kern = functools.partial(_projection_kernel, rope=rope, full=full, tm=tm, n_tiles=n_tiles)
    if full:
        in_specs = [*tile_specs(lambda b, i: (b, i)), _layer_spec((1, D_MODEL), layer),
                    _layer_spec((D_MODEL, PROJ_WIDTH), layer), gain_spec, gain_spec,
                    _const_spec(p["ones"].shape), tab_spec, tab_spec, tab_spec,
                    _layer_spec(p["conv_w"].shape[1:], layer), _layer_spec((1, CONV_WIDTH), layer),
                    _layer_spec((CONV_WIDTH, D_MODEL), layer), _layer_spec(p["b_gate"].shape[1:], layer)]
        args = (x, x, x, mod, mod, p["norm_g"], p["w_in"], p["q_gain"], p["k_gain"], p["ones"], *rope_tabs,
                p["conv_w"], p["conv_b"], p["w_conv_br"], p["b_gate"])
        out_specs = [qt_spec, k_spec, vt_spec, row_spec, row_spec, row_spec]
        out_shape = [jax.ShapeDtypeStruct((bsz, ATTN_WIDTH, seq), BF16), k_shape, vt_shape,
                     jax.ShapeDtypeStruct((bsz, seq, ATTN_WIDTH), BF16),
                     jax.ShapeDtypeStruct((bsz, seq, D_MODEL), BF16),
                     jax.ShapeDtypeStruct((bsz, seq, D_MODEL), F32)]
        scratch = [pltpu.VMEM((tm + 2 * HALO, D_MODEL), BF16),
                   pltpu.VMEM((tm + 2 * HALO, V7X_MXU_COLS), F32),
                   pltpu.VMEM((tm, CONV_WIDTH), BF16)]
    else:
        kv_cols = 2 * KV_WIDTH
        x_spec, _, _, shift_spec, scale_spec = tile_specs(lambda b, i: (b, i))
        in_specs = [x_spec, shift_spec, scale_spec, _layer_spec((1, D_MODEL), layer),
                    _layer_spec((D_MODEL, kv_cols), layer, col_block=OFF_K // kv_cols),
                    gain_spec, _const_spec(p["ones"].shape), tab_spec, tab_spec, tab_spec]
        args = (x, mod, mod, p["norm_g"], p["w_in"], p["k_gain"], p["ones"], *rope_tabs)
        out_specs = [k_spec, vt_spec]
        out_shape = [k_shape, vt_shape]
        scratch = [pltpu.VMEM((tm, D_MODEL), BF16)]
    return pl.pallas_call(
        kern,
        grid=(bsz, n_tiles),
        in_specs=in_specs,
        out_specs=out_specs,
        out_shape=out_shape,
        scratch_shapes=scratch,
        compiler_params=pltpu.CompilerParams(dimension_semantics=("arbitrary", "arbitrary"),
                                             vmem_limit_bytes=vmem_limit),
        name="projection_full" if full else "projection_kv",
    )(*args)


def _attention_kernel(*refs, n_seg):
    slot0_ref, qt_ref = refs[0], refs[1]
    refs = refs[1:]
    k_refs, vt_refs = refs[1:1 + n_seg], refs[1 + n_seg:1 + 2 * n_seg]
    (sza_ref, g0_ref, t1_ref, x_ref, gate_ref, wab_ref, wout_ref, o_ref, attn_scr, st_scr) = refs[1 + 2 * n_seg:]
    slot0 = slot0_ref[0]
    tiles = [(seg, t) for seg in range(n_seg) for t in range(k_refs[seg].shape[2] // KEY_TILE)]
    n_tiles = len(tiles)
    group = N_Q_HEADS // N_KV_HEADS
    sublanes = 8
    slots = st_scr.shape[0]
    tq = st_scr.shape[2]
    units = [(sub, h) for sub in range(qt_ref.shape[2] // tq) for h in range(N_Q_HEADS)]

    def score_tile(g, t, m8):
        sub, h = units[g]
        seg, local = tiles[t]
        s = _dot(k_refs[seg][0, h // group, local * KEY_TILE:(local + 1) * KEY_TILE, :],
                 qt_ref[0, h * HEAD_DIM:(h + 1) * HEAD_DIM, sub * tq:(sub + 1) * tq])
        st_scr[slot0 + g % slots, t * KEY_TILE:(t + 1) * KEY_TILE, :] = s.astype(BF16)
        for r in range(KEY_TILE // sublanes):
            piece = s[r * sublanes:(r + 1) * sublanes]
            m8 = piece if m8 is None else jnp.maximum(m8, piece)
        return m8

    def pv_tile(g, t, m_row, acc):
        seg, local = tiles[t]
        kv = units[g][1] // group
        p = jnp.exp2(st_scr[slot0 + g % slots, t * KEY_TILE:(t + 1) * KEY_TILE, :] - m_row)
        part = _dot(vt_refs[seg][0, kv, :, local * KEY_TILE:(local + 1) * KEY_TILE], p)
        return part if acc is None else acc + part

    def merge_stages(sub):
        rows = slice(sub * tq, (sub + 1) * tq)
        state = {}

        def branch():
            state["br"] = _dot(attn_scr[sub], wab_ref[...])

        def project():
            merged = g0_ref[0, rows, :].astype(F32) * state.pop("br") + t1_ref[0, rows, :]
            state["out"] = _dot(merged.astype(BF16), wout_ref[...])

        def residual():
            o_ref[0, rows, :] = x_ref[0, rows, :] + gate_ref[0] * state.pop("out")

        return [branch, project, residual]

    col_max = {}
    for g in range(min(SCORE_LOOKAHEAD, len(units))):
        m8 = None
        for t in range(n_tiles):
            m8 = score_tile(g, t, m8)
        col_max[g] = jnp.max(m8, axis=0, keepdims=True).astype(BF16)
    halves, pending = [], []
    for g, (sub, h) in enumerate(units):
        if pending:
            pending.pop(0)()
        acc, m8 = None, None
        ahead = g + SCORE_LOOKAHEAD
        m_row = col_max.pop(g)
        for t in range(n_tiles):
            if ahead < len(units):
                m8 = score_tile(ahead, t, m8)
            acc = pv_tile(g, t, m_row, acc)
        if ahead < len(units):
            col_max[ahead] = jnp.max(m8, axis=0, keepdims=True).astype(BF16)
        halves.append(acc[:HEAD_DIM] / acc[HEAD_DIM:HEAD_DIM + 1])
        if len(halves) == V7X_LANES // HEAD_DIM:
            j = h // len(halves)
            lanes = slice(j * V7X_LANES, (j + 1) * V7X_LANES)
            attn = jnp.concatenate(halves, axis=0).T
            gate = sza_ref[0, sub * tq:(sub + 1) * tq, lanes].astype(F32)
            attn_scr[sub, :, lanes] = (attn * gate).astype(BF16)
            halves = []
        if h == N_Q_HEADS - 1:
            pending += merge_stages(sub)
    for stage in pending:
        stage()


def _attention(qt, ks, vts, sza, g0, t1, x, mod, mod_row, layer, w_attn_br, w_out):
    bsz, seq, _ = x.shape
    sub_rows = min(ROW_TILE, seq)
    tq = min(ATTN_SUBTILES * sub_rows, seq)
    n_keys = sum(k.shape[2] for k in ks)
    mod_base = layer * MOD_ROWS
    k_specs = [pl.BlockSpec((1, N_KV_HEADS, k.shape[2], HEAD_DIM), lambda b, i: (b, 0, 0, 0)) for k in ks]
    vt_specs = [pl.BlockSpec((1, N_KV_HEADS, PV_ROWS, vt.shape[3]), lambda b, i: (b, 0, 0, 0)) for vt in vts]
    row_spec = pl.BlockSpec((1, tq, D_MODEL), lambda b, i: (b, i, 0))
    if mod_row is None:
        gate_spec = pl.BlockSpec((1, 1, D_MODEL), lambda b, i: (mod_base + b, 0, 2))
    else:
        gate_spec = pl.BlockSpec((1, 1, D_MODEL), lambda b, i: (mod_base + mod_row, 0, 2))
    return pl.pallas_call(
        functools.partial(_attention_kernel, n_seg=len(ks)),
        grid=(bsz, seq // tq),
        in_specs=[pl.BlockSpec(memory_space=pltpu.SMEM),
                  pl.BlockSpec((1, ATTN_WIDTH, tq), lambda b, i: (b, 0, i)), *k_specs, *vt_specs,
                  row_spec, row_spec, row_spec, row_spec, gate_spec,
                  _layer_spec(w_attn_br.shape[1:], layer), _layer_spec(w_out.shape[1:], layer)],
        out_specs=row_spec,
        out_shape=jax.ShapeDtypeStruct(x.shape, F32),
        scratch_shapes=[pltpu.VMEM((tq // sub_rows, sub_rows, ATTN_WIDTH), BF16),
                        pltpu.VMEM((SCORE_LOOKAHEAD + 1, n_keys, sub_rows), BF16)],
        compiler_params=pltpu.CompilerParams(dimension_semantics=("arbitrary", "arbitrary"),
                                             vmem_limit_bytes=V7X_VMEM_BYTES * 3 // 4),
        name="attention_merge",
    )(jnp.zeros((1,), jnp.int32), qt, *ks, *vts, sza, g0, t1, x, mod, w_attn_br, w_out)


def _rope_tables(n_tokens):
    t = np.arange(n_tokens)
    row = (t // GRID_W).astype(np.float32)
    col = (t % GRID_W).astype(np.float32)
    half = HEAD_DIM // 2
    inv = (np.float32(ROPE_THETA) ** (-np.arange(0, half, 2, dtype=np.float32) / np.float32(half))).astype(np.float32)
    ang_r = row[:, None] * inv[None, :]
    ang_c = col[:, None] * inv[None, :]
    ang = np.concatenate([ang_r, ang_r, ang_c, ang_c], axis=-1).astype(np.float64)
    cos, sin = np.cos(ang), np.sin(ang)
    first = (np.arange(HEAD_DIM) % half) < half // 2
    sin_lo = np.where(first[None, :], -sin, 0.0)
    sin_hi = np.where(first[None, :], 0.0, sin)
    reps = V7X_LANES // HEAD_DIM
    return tuple(jnp.asarray(np.tile(a, (1, reps)), dtype=F32) for a in (cos, sin_lo, sin_hi))


def _identity_rope_tables(n_tokens):
    one = jnp.ones((n_tokens, V7X_LANES), F32)
    zero = jnp.zeros((n_tokens, V7X_LANES), F32)
    return one, zero, zero


def _head_sum_matrix():
    idx = np.arange(V7X_MXU_COLS) // HEAD_DIM
    return jnp.asarray(idx[:, None] == idx[None, :], dtype=BF16)


def kernel(x, c, ctx, c_ctx, norm_g, w_ada, b_ada, w_in, q_norm_g, k_norm_g, conv_w, conv_b, w_attn_br, w_conv_br,
           b_gate, w_out):
    depth = w_ada.shape[0]
    bsz, seq, _ = x.shape
    ctx_len = ctx.shape[1]
    assert bsz + 1 <= MOD_ROWS and seq % ROW_TILE == 0 and ctx_len % HALO == 0 and ctx_len <= ROW_TILE

    cond = jnp.concatenate([c, c_ctx[None, :], jnp.zeros((MOD_ROWS - bsz - 1, D_MODEL), F32)], axis=0)
    mod = _modulation(cond, w_ada, b_ada).reshape(depth * MOD_ROWS, 1, 3 * D_MODEL)
    ctx_row = bsz

    rope_x = _rope_tables(seq)
    rope_c = _identity_rope_tables(ctx_len)
    reps = V7X_MXU_COLS // HEAD_DIM
    p = dict(
        norm_g=norm_g.reshape(depth, 1, D_MODEL),
        w_in=w_in.astype(BF16),
        q_gain=jnp.tile(q_norm_g, (1, reps)).reshape(depth, 1, V7X_MXU_COLS),
        k_gain=jnp.tile(k_norm_g, (1, reps)).reshape(depth, 1, V7X_MXU_COLS),
        ones=_head_sum_matrix(),
        conv_w=conv_w,
        conv_b=conv_b.reshape(depth, 1, CONV_WIDTH),
        w_conv_br=w_conv_br.astype(BF16),
        b_gate=b_gate,
    )
    wab = w_attn_br.astype(BF16)
    wo = w_out.astype(BF16)

    for l in range(depth):
        last = l == depth - 1
        qx, ktx, vx, szax, g0x, t1x = _projection(x, mod, None, l, p, rope_x, rope=True, full=True)
        if last:
            ktc, vc = _projection(ctx, mod, ctx_row, l, p, rope_c, rope=False, full=False)
        else:
            qc, ktc, vc, szac, g0c, t1c = _projection(ctx, mod, ctx_row, l, p, rope_c, rope=False, full=True)
        x_new = _attention(qx, [ktc, ktx], [vc, vx], szax, g0x, t1x, x, mod, None, l, wab, wo)
        if not last:
            ctx = _attention(qc, [ktc], [vc], szac, g0c, t1c, ctx, mod, ctx_row, l, wab, wo)
        x = x_new
    return x
```

```python
import functools

import numpy as np
import jax
import jax.numpy as jnp
from jax import lax
from jax.experimental import pallas as pl
from jax.experimental.pallas import tpu as pltpu

D_MODEL = 1024
GRID_W = 64
HEAD_DIM = 64
N_Q_HEADS = 16
N_KV_HEADS = 4
ATTN_WIDTH = N_Q_HEADS * HEAD_DIM
KV_WIDTH = N_KV_HEADS * HEAD_DIM
CONV_WIDTH = D_MODEL
ROPE_THETA = 10000.0
EPS = 1e-6
ATTN_SCALE = HEAD_DIM ** -0.5
SCORE_SCALE = ATTN_SCALE * 1.4426950408889634

OFF_Q = 0
OFF_K = OFF_Q + ATTN_WIDTH
OFF_V = OFF_K + KV_WIDTH
OFF_ZA = OFF_V + KV_WIDTH
OFF_XC = OFF_ZA + ATTN_WIDTH
OFF_BC = OFF_XC + CONV_WIDTH
OFF_CC = OFF_BC + CONV_WIDTH
OFF_ZC = OFF_CC + CONV_WIDTH
OFF_GL = OFF_ZC + CONV_WIDTH
PROJ_WIDTH = OFF_GL + 2 * D_MODEL

V7X_LANES = 128
V7X_MXU_COLS = 256
V7X_BF16_SUBLANES = 16
V7X_VMEM_BYTES = 64 * 1024 * 1024
HALO = V7X_BF16_SUBLANES
ROW_TILE = 256
PV_ROWS = HEAD_DIM + V7X_BF16_SUBLANES
KEY_TILE = V7X_MXU_COLS
SCORE_LOOKAHEAD = 2
ATTN_SUBTILES = 4
MOD_ROWS = 16
MOD_COLS = 512

BF16 = jnp.bfloat16
F32 = jnp.float32


def _dot(a, b):
    return jnp.dot(a, b, preferred_element_type=F32)


def _silu(x):
    return x * jax.nn.sigmoid(x)


def _modulation_kernel(cond_ref, w_ref, b_ref, o_ref):
    a = _silu(cond_ref[...]).astype(BF16)
    o_ref[0] = _dot(a, w_ref[0].astype(BF16)) + b_ref[0]


def _modulation(cond, w_ada, b_ada):
    depth = w_ada.shape[0]
    n_out = w_ada.shape[2]
    return pl.pallas_call(
        _modulation_kernel,
        grid=(depth, n_out // MOD_COLS),
        in_specs=[
            pl.BlockSpec((MOD_ROWS, D_MODEL), lambda l, j: (0, 0)),
            pl.BlockSpec((1, D_MODEL, MOD_COLS), lambda l, j: (l, 0, j)),
            pl.BlockSpec((1, 1, MOD_COLS), lambda l, j: (l, 0, j)),
        ],
        out_specs=pl.BlockSpec((1, MOD_ROWS, MOD_COLS), lambda l, j: (l, 0, j)),
        out_shape=jax.ShapeDtypeStruct((depth, MOD_ROWS, n_out), F32),
        name="modulation",
    )(cond, w_ada, b_ada.reshape(depth, 1, n_out))


def _head_norm_rope(raw, gain, ones_ref, rope_refs, row0, tm, out_scale):
    sumsq = _dot((raw * raw).astype(BF16), ones_ref[...])
    r = lax.rsqrt(sumsq * (1.0 / HEAD_DIM) + EPS)
    a = raw * gain
    if rope_refs is None:
        return a * (r * out_scale) if out_scale != 1.0 else a * r
    cos_ref, sin_lo_ref, sin_hi_ref = rope_refs
    rows = pl.ds(row0, tm)
    cos, sin_lo, sin_hi = cos_ref[rows, :], sin_lo_ref[rows, :], sin_hi_ref[rows, :]
    parts = []
    for m in range(V7X_MXU_COLS // V7X_LANES):
        c = a[:, m * V7X_LANES:(m + 1) * V7X_LANES]
        up = pltpu.roll(c, V7X_LANES - HEAD_DIM // 4, 1)
        down = pltpu.roll(c, HEAD_DIM // 4, 1)
        parts.append(c * cos + up * sin_lo + down * sin_hi)
    rot = jnp.concatenate(parts, axis=1)
    return rot * (r * out_scale) if out_scale != 1.0 else rot * r


def _projection_kernel(*refs, rope, full, tm, n_tiles):
    if full:
        (x_ref, xp_ref, xn_ref, shift_ref, scale_ref, ng_ref, w_ref, qg_ref, kg_ref, ones_ref,
         cos_ref, sin_lo_ref, sin_hi_ref, cw_ref, cb_ref, wcb_ref, bg_ref,
         qt_out, k_out, vt_out, sza_out, g0_out, t1_out, h_scr, u_scr, yz_scr) = refs
    else:
        (x_ref, shift_ref, scale_ref, ng_ref, w_ref, kg_ref, ones_ref,
         cos_ref, sin_lo_ref, sin_hi_ref, k_out, vt_out, h_scr) = refs
    i = pl.program_id(1)
    rope_refs = (cos_ref, sin_lo_ref, sin_hi_ref) if rope else None
    row0 = pl.multiple_of(i * tm, tm)

    mod_gain = ng_ref[...] * (1.0 + scale_ref[0])
    mod_shift = shift_ref[0]

    def modulated(xv):
        ms = jnp.mean(xv * xv, axis=-1, keepdims=True)
        return ((xv * lax.rsqrt(ms + EPS)) * mod_gain + mod_shift).astype(BF16)

    if full:
        h_scr[0:HALO, :] = modulated(xp_ref[0])
        h_scr[HALO:HALO + tm, :] = modulated(x_ref[0])
        h_scr[HALO + tm:, :] = modulated(xn_ref[0])
        h_main = h_scr[HALO:HALO + tm, :]
    else:
        h_scr[...] = modulated(x_ref[0])
        h_main = h_scr[...]

    def w_cols(off, g):
        return w_ref[:, off + g * V7X_MXU_COLS: off + (g + 1) * V7X_MXU_COLS]

    def group(off, g):
        return _dot(h_main, w_cols(off, g))

    k_w = w_cols(OFF_K, 0) if full else w_ref[:, 0:KV_WIDTH]
    v_w = w_cols(OFF_V, 0) if full else w_ref[:, KV_WIDTH:2 * KV_WIDTH]
    k = _head_norm_rope(_dot(h_main, k_w), kg_ref[...], ones_ref, rope_refs, row0, tm, 1.0)
    v = _dot(h_main, v_w)
    for m in range(KV_WIDTH // V7X_LANES):
        lanes = slice(m * V7X_LANES, (m + 1) * V7X_LANES)
        k_pair = k[:, lanes]
        k_out[0, 2 * m] = k_pair[:, :HEAD_DIM].astype(BF16)
        k_out[0, 2 * m + 1] = pltpu.roll(k_pair, HEAD_DIM, 1)[:, :HEAD_DIM].astype(BF16)
        vt_pair = v[:, lanes].T.astype(BF16)
        vt_out[0, 2 * m, 0:HEAD_DIM, :] = vt_pair[:HEAD_DIM]
        vt_out[0, 2 * m + 1, 0:HEAD_DIM, :] = vt_pair[HEAD_DIM:]
    vt_out[0, :, HEAD_DIM:, :] = jnp.ones((N_KV_HEADS, PV_ROWS - HEAD_DIM, tm), BF16)
    if not full:
        return

    n_groups = ATTN_WIDTH // V7X_MXU_COLS
    for g in range(n_groups):
        cols = slice(g * V7X_MXU_COLS, (g + 1) * V7X_MXU_COLS)
        q = _head_norm_rope(group(OFF_Q, g), qg_ref[...], ones_ref, rope_refs, row0, tm, SCORE_SCALE)
        for m in range(V7X_MXU_COLS // V7X_LANES):
            r0 = g * V7X_MXU_COLS + m * V7X_LANES
            qt_out[0, r0:r0 + V7X_LANES, :] = q[:, m * V7X_LANES:(m + 1) * V7X_LANES].T.astype(BF16)
        sza_out[0, :, cols] = _silu(group(OFF_ZA, g)).astype(BF16)

    h_ext = h_scr[...]
    row = lax.broadcasted_iota(jnp.int32, (tm, V7X_MXU_COLS), 0)
    first_row = jnp.logical_and(row == 0, i == 0)
    last_row = jnp.logical_and(row == tm - 1, i == n_tiles - 1)
    for g in range(CONV_WIDTH // V7X_MXU_COLS):
        cols = slice(g * V7X_MXU_COLS, (g + 1) * V7X_MXU_COLS)
        u_scr[...] = _dot(h_ext, w_cols(OFF_CC, g)) * _dot(h_ext, w_cols(OFF_XC, g))
        u_prev = jnp.where(first_row, 0.0, u_scr[HALO - 1:HALO - 1 + tm, :])
        u_next = jnp.where(last_row, 0.0, u_scr[HALO + 1:HALO + 1 + tm, :])
        conv = (u_prev * cw_ref[0:1, cols] + u_scr[HALO:HALO + tm, :] * cw_ref[1:2, cols]
                + u_next * cw_ref[2:3, cols] + cb_ref[:, cols])
        y = group(OFF_BC, g) * conv
        yz_scr[:, cols] = (y * _silu(group(OFF_ZC, g))).astype(BF16)

    yz = yz_scr[...]
    for g in range(D_MODEL // V7X_MXU_COLS):
        cols = slice(g * V7X_MXU_COLS, (g + 1) * V7X_MXU_COLS)
        g0_out[0, :, cols] = jax.nn.sigmoid(group(OFF_GL, g) + bg_ref[0:1, cols]).astype(BF16)
        gate1 = jax.nn.sigmoid(group(OFF_GL + D_MODEL, g) + bg_ref[1:2, cols])
        t1_out[0, :, cols] = gate1 * _dot(yz, wcb_ref[:, cols])


def _const_spec(shape):
    return pl.BlockSpec(shape, lambda b, i: (0,) * len(shape), pipeline_mode=pl.Buffered(1))


def _layer_spec(shape, layer, col_block=0):
    zeros = (0,) * (len(shape) - 1)
    return pl.BlockSpec((None, *shape), lambda b, i: (layer, *zeros, col_block), pipeline_mode=pl.Buffered(1))


def _projection(x, mod, mod_row, layer, p, rope_tabs, *, rope, full):
    bsz, seq, _ = x.shape
    tm = min(2 * ROW_TILE, seq)
    n_tiles = seq // tm
    halo_per_tile = tm // HALO
    n_halo = seq // HALO
    mod_base = layer * MOD_ROWS

    def tile_specs(tile_of):
        def mod_spec(part):
            if mod_row is None:
                return pl.BlockSpec((1, 1, D_MODEL), lambda b, i: (mod_base + tile_of(b, i)[0], 0, part))
            return pl.BlockSpec((1, 1, D_MODEL), lambda b, i: (mod_base + mod_row, 0, part))

        def prev_halo(b, i):
            tb, ti = tile_of(b, i)
            return tb, jnp.maximum(ti * halo_per_tile - 1, 0), 0

        def next_halo(b, i):
            tb, ti = tile_of(b, i)
            return tb, jnp.minimum((ti + 1) * halo_per_tile, n_halo - 1), 0

        return [pl.BlockSpec((1, tm, D_MODEL), lambda b, i: (*tile_of(b, i), 0)),
                pl.BlockSpec((1, HALO, D_MODEL), prev_halo), pl.BlockSpec((1, HALO, D_MODEL), next_halo),
                mod_spec(0), mod_spec(1)]


    tab_spec = _const_spec(rope_tabs[0].shape)
    k_spec = pl.BlockSpec((1, N_KV_HEADS, tm, HEAD_DIM), lambda b, i: (b, 0, i, 0))
    vt_spec = pl.BlockSpec((1, N_KV_HEADS, PV_ROWS, tm), lambda b, i: (b, 0, 0, i))
    k_shape = jax.ShapeDtypeStruct((bsz, N_KV_HEADS, seq, HEAD_DIM), BF16)
    vt_shape = jax.ShapeDtypeStruct((bsz, N_KV_HEADS, PV_ROWS, seq), BF16)
    qt_spec = pl.BlockSpec((1, ATTN_WIDTH, tm), lambda b, i: (b, 0, i))
    row_spec = pl.BlockSpec((1, tm, D_MODEL), lambda b, i: (b, i, 0))
    vmem_limit = V7X_VMEM_BYTES * 7 // 8
    gain_spec = _layer_spec((1, V7X_MXU_COLS), layer)

    kern = functools.partial(_projection_kernel, rope=rope, full=full, tm=tm, n_tiles=n_tiles)
    if full:
        in_specs = [*tile_specs(lambda b, i: (b, i)), _layer_spec((1, D_MODEL), layer),
                    _layer_spec((D_MODEL, PROJ_WIDTH), layer), gain_spec, gain_spec,
                    _const_spec(p["ones"].shape), tab_spec, tab_spec, tab_spec,
                    _layer_spec(p["conv_w"].shape[1:], layer), _layer_spec((1, CONV_WIDTH), layer),
                    _layer_spec((CONV_WIDTH, D_MODEL), layer), _layer_spec(p["b_gate"].shape[1:], layer)]
        args = (x, x, x, mod, mod, p["norm_g"], p["w_in"], p["q_gain"], p["k_gain"], p["ones"], *rope_tabs,
                p["conv_w"], p["conv_b"], p["w_conv_br"], p["b_gate"])
        out_specs = [qt_spec, k_spec, vt_spec, row_spec, row_spec, row_spec]
        out_shape = [jax.ShapeDtypeStruct((bsz, ATTN_WIDTH, seq), BF16), k_shape, vt_shape,
                     jax.ShapeDtypeStruct((bsz, seq, ATTN_WIDTH), BF16),
                     jax.ShapeDtypeStruct((bsz, seq, D_MODEL), BF16),
                     jax.ShapeDtypeStruct((bsz, seq, D_MODEL), F32)]
        scratch = [pltpu.VMEM((tm + 2 * HALO, D_MODEL), BF16),
                   pltpu.VMEM((tm + 2 * HALO, V7X_MXU_COLS), F32),
                   pltpu.VMEM((tm, CONV_WIDTH), BF16)]
    else:
        kv_cols = 2 * KV_WIDTH
        x_spec, _, _, shift_spec, scale_spec = tile_specs(lambda b, i: (b, i))
        in_specs = [x_spec, shift_spec, scale_spec, _layer_spec((1, D_MODEL), layer),
                    _layer_spec((D_MODEL, kv_cols), layer, col_block=OFF_K // kv_cols),
                    gain_spec, _const_spec(p["ones"].shape), tab_spec, tab_spec, tab_spec]
        args = (x, mod, mod, p["norm_g"], p["w_in"], p["k_gain"], p["ones"], *rope_tabs)
        out_specs = [k_spec, vt_spec]
        out_shape = [k_shape, vt_shape]
        scratch = [pltpu.VMEM((tm, D_MODEL), BF16)]
    return pl.pallas_call(
        kern,
        grid=(bsz, n_tiles),
        in_specs=in_specs,
        out_specs=out_specs,
        out_shape=out_shape,
        scratch_shapes=scratch,
        compiler_params=pltpu.CompilerParams(dimension_semantics=("arbitrary", "arbitrary"),
                                             vmem_limit_bytes=vmem_limit),
        name="projection_full" if full else "projection_kv",
    )(*args)


def _attention_kernel(*refs, n_seg):
    slot0_ref, qt_ref = refs[0], refs[1]
    refs = refs[1:]
    k_refs, vt_refs = refs[1:1 + n_seg], refs[1 + n_seg:1 + 2 * n_seg]
    (sza_ref, g0_ref, t1_ref, x_ref, gate_ref, wab_ref, wout_ref, o_ref, attn_scr, st_scr) = refs[1 + 2 * n_seg:]
    slot0 = slot0_ref[0]
    tiles = [(seg, t) for seg in range(n_seg) for t in range(k_refs[seg].shape[2] // KEY_TILE)]
    n_tiles = len(tiles)
    group = N_Q_HEADS // N_KV_HEADS
    sublanes = 8
    slots = st_scr.shape[0]
    tq = st_scr.shape[2]
    units = [(sub, h) for sub in range(qt_ref.shape[2] // tq) for h in range(N_Q_HEADS)]

    def score_tile(g, t, m8):
        sub, h = units[g]
        seg, local = tiles[t]
        s = _dot(k_refs[seg][0, h // group, local * KEY_TILE:(local + 1) * KEY_TILE, :],
                 qt_ref[0, h * HEAD_DIM:(h + 1) * HEAD_DIM, sub * tq:(sub + 1) * tq])
        st_scr[slot0 + g % slots, t * KEY_TILE:(t + 1) * KEY_TILE, :] = s.astype(BF16)
        for r in range(KEY_TILE // sublanes):
            piece = s[r * sublanes:(r + 1) * sublanes]
            m8 = piece if m8 is None else jnp.maximum(m8, piece)
        return m8

    def pv_tile(g, t, m_row, acc):
        seg, local = tiles[t]
        kv = units[g][1] // group
        p = jnp.exp2(st_scr[slot0 + g % slots, t * KEY_TILE:(t + 1) * KEY_TILE, :] - m_row)
        part = _dot(vt_refs[seg][0, kv, :, local * KEY_TILE:(local + 1) * KEY_TILE], p)
        return part if acc is None else acc + part

    def merge_stages(sub):
        rows = slice(sub * tq, (sub + 1) * tq)
        state = {}

        def branch():
            state["br"] = _dot(attn_scr[sub], wab_ref[...])

        def project():
            merged = g0_ref[0, rows, :].astype(F32) * state.pop("br") + t1_ref[0, rows, :]
            state["out"] = _dot(merged.astype(BF16), wout_ref[...])

        def residual():
            o_ref[0, rows, :] = x_ref[0, rows, :] + gate_ref[0] * state.pop("out")

        return [branch, project, residual]

    col_max = {}
    for g in range(min(SCORE_LOOKAHEAD, len(units))):
        m8 = None
        for t in range(n_tiles):
            m8 = score_tile(g, t, m8)
        col_max[g] = jnp.max(m8, axis=0, keepdims=True).astype(BF16)
    halves, pending = [], []
    for g, (sub, h) in enumerate(units):
        if pending:
            pending.pop(0)()
        acc, m8 = None, None
        ahead = g + SCORE_LOOKAHEAD
        m_row = col_max.pop(g)
        for t in range(n_tiles):
            if ahead < len(units):
                m8 = score_tile(ahead, t, m8)
            acc = pv_tile(g, t, m_row, acc)
        if ahead < len(units):
            col_max[ahead] = jnp.max(m8, axis=0, keepdims=True).astype(BF16)
        halves.append(acc[:HEAD_DIM] / acc[HEAD_DIM:HEAD_DIM + 1])
        if len(halves) == V7X_LANES // HEAD_DIM:
            j = h // len(halves)
            lanes = slice(j * V7X_LANES, (j + 1) * V7X_LANES)
            attn = jnp.concatenate(halves, axis=0).T
            gate = sza_ref[0, sub * tq:(sub + 1) * tq, lanes].astype(F32)
            attn_scr[sub, :, lanes] = (attn * gate).astype(BF16)
            halves = []
        if h == N_Q_HEADS - 1:
            pending += merge_stages(sub)
    for stage in pending:
        stage()


def _attention(qt, ks, vts, sza, g0, t1, x, mod, mod_row, layer, w_attn_br, w_out):
    bsz, seq, _ = x.shape
    sub_rows = min(ROW_TILE, seq)
    tq = min(ATTN_SUBTILES * sub_rows, seq)
    n_keys = sum(k.shape[2] for k in ks)
    mod_base = layer * MOD_ROWS
    k_specs = [pl.BlockSpec((1, N_KV_HEADS, k.shape[2], HEAD_DIM), lambda b, i: (b, 0, 0, 0)) for k in ks]
    vt_specs = [pl.BlockSpec((1, N_KV_HEADS, PV_ROWS, vt.shape[3]), lambda b, i: (b, 0, 0, 0)) for vt in vts]
    row_spec = pl.BlockSpec((1, tq, D_MODEL), lambda b, i: (b, i, 0))
    if mod_row is None:
        gate_spec = pl.BlockSpec((1, 1, D_MODEL), lambda b, i: (mod_base + b, 0, 2))
    else:
        gate_spec = pl.BlockSpec((1, 1, D_MODEL), lambda b, i: (mod_base + mod_row, 0, 2))
    return pl.pallas_call(
        functools.partial(_attention_kernel, n_seg=len(ks)),
        grid=(bsz, seq // tq),
        in_specs=[pl.BlockSpec(memory_space=pltpu.SMEM),
                  pl.BlockSpec((1, ATTN_WIDTH, tq), lambda b, i: (b, 0, i)), *k_specs, *vt_specs,
                  row_spec, row_spec, row_spec, row_spec, gate_spec,
                  _layer_spec(w_attn_br.shape[1:], layer), _layer_spec(w_out.shape[1:], layer)],
        out_specs=row_spec,
        out_shape=jax.ShapeDtypeStruct(x.shape, F32),
        scratch_shapes=[pltpu.VMEM((tq // sub_rows, sub_rows, ATTN_WIDTH), BF16),
                        pltpu.VMEM((SCORE_LOOKAHEAD + 1, n_keys, sub_rows), BF16)],
        compiler_params=pltpu.CompilerParams(dimension_semantics=("arbitrary", "arbitrary"),
                                             vmem_limit_bytes=V7X_VMEM_BYTES * 7 // 8),
        name="attention_merge",
    )(jnp.zeros((1,), jnp.int32), qt, *ks, *vts, sza, g0, t1, x, mod, w_attn_br, w_out)


def _rope_tables(n_tokens):
    t = np.arange(n_tokens)
    row = (t // GRID_W).astype(np.float32)
    col = (t % GRID_W).astype(np.float32)
    half = HEAD_DIM // 2
    inv = (np.float32(ROPE_THETA) ** (-np.arange(0, half, 2, dtype=np.float32) / np.float32(half))).astype(np.float32)
    ang_r = row[:, None] * inv[None, :]
    ang_c = col[:, None] * inv[None, :]
    ang = np.concatenate([ang_r, ang_r, ang_c, ang_c], axis=-1).astype(np.float64)
    cos, sin = np.cos(ang), np.sin(ang)
    first = (np.arange(HEAD_DIM) % half) < half // 2
    sin_lo = np.where(first[None, :], -sin, 0.0)
    sin_hi = np.where(first[None, :], 0.0, sin)
    reps = V7X_LANES // HEAD_DIM
    return tuple(jnp.asarray(np.tile(a, (1, reps)), dtype=F32) for a in (cos, sin_lo, sin_hi))


def _identity_rope_tables(n_tokens):
    one = jnp.ones((n_tokens, V7X_LANES), F32)
    zero = jnp.zeros((n_tokens, V7X_LANES), F32)
    return one, zero, zero


def _head_sum_matrix():
    idx = np.arange(V7X_MXU_COLS) // HEAD_DIM
    return jnp.asarray(idx[:, None] == idx[None, :], dtype=BF16)


def kernel(x, c, ctx, c_ctx, norm_g, w_ada, b_ada, w_in, q_norm_g, k_norm_g, conv_w, conv_b, w_attn_br, w_conv_br,
           b_gate, w_out):
    depth = w_ada.shape[0]
    bsz, seq, _ = x.shape
    ctx_len = ctx.shape[1]
    assert bsz + 1 <= MOD_ROWS and seq % ROW_TILE == 0 and ctx_len % HALO == 0 and ctx_len <= ROW_TILE

    cond = jnp.concatenate([c, c_ctx[None, :], jnp.zeros((MOD_ROWS - bsz - 1, D_MODEL), F32)], axis=0)
    mod = _modulation(cond, w_ada, b_ada).reshape(depth * MOD_ROWS, 1, 3 * D_MODEL)
    ctx_row = bsz

    rope_x = _rope_tables(seq)
    rope_c = _identity_rope_tables(ctx_len)
    reps = V7X_MXU_COLS // HEAD_DIM
    p = dict(
        norm_g=norm_g.reshape(depth, 1, D_MODEL),
        w_in=w_in.astype(BF16),
        q_gain=jnp.tile(q_norm_g, (1, reps)).reshape(depth, 1, V7X_MXU_COLS),
        k_gain=jnp.tile(k_norm_g, (1, reps)).reshape(depth, 1, V7X_MXU_COLS),
        ones=_head_sum_matrix(),
        conv_w=conv_w,
        conv_b=conv_b.reshape(depth, 1, CONV_WIDTH),
        w_conv_br=w_conv_br.astype(BF16),
        b_gate=b_gate,
    )
    wab = w_attn_br.astype(BF16)
    wo = w_out.astype(BF16)

    for l in range(depth):
        last = l == depth - 1
        qx, ktx, vx, szax, g0x, t1x = _projection(x, mod, None, l, p, rope_x, rope=True, full=True)
        if last:
            ktc, vc = _projection(ctx, mod, ctx_row, l, p, rope_c, rope=False, full=False)
        else:
            qc, ktc, vc, szac, g0c, t1c = _projection(ctx, mod, ctx_row, l, p, rope_c, rope=False, full=True)
        x_new = _attention(qx, [ktc, ktx], [vc, vx], szax, g0x, t1x, x, mod, None, l, wab, wo)
        if not last:
            ctx = _attention(qc, [ktc], [vc], szac, g0c, t1c, ctx, mod, ctx_row, l, wab, wo)
        x = x_new
    return x
```

```python
import functools

import numpy as np
import jax
import jax.numpy as jnp
from jax import lax
from jax.experimental import pallas as pl
from jax.experimental.pallas import tpu as pltpu

D_MODEL = 1024
GRID_W = 64
HEAD_DIM = 64
N_Q_HEADS = 16
N_KV_HEADS = 4
ATTN_WIDTH = N_Q_HEADS * HEAD_DIM
KV_WIDTH = N_KV_HEADS * HEAD_DIM
CONV_WIDTH = D_MODEL
ROPE_THETA = 10000.0
EPS = 1e-6
ATTN_SCALE = HEAD_DIM ** -0.5
SCORE_SCALE = ATTN_SCALE * 1.4426950408889634

OFF_Q = 0
OFF_K = OFF_Q + ATTN_WIDTH
OFF_V = OFF_K + KV_WIDTH
OFF_ZA = OFF_V + KV_WIDTH
OFF_XC = OFF_ZA + ATTN_WIDTH
OFF_BC = OFF_XC + CONV_WIDTH
OFF_CC = OFF_BC + CONV_WIDTH
OFF_ZC = OFF_CC + CONV_WIDTH
OFF_GL = OFF_ZC + CONV_WIDTH
PROJ_WIDTH = OFF_GL + 2 * D_MODEL

V7X_LANES = 128
V7X_MXU_COLS = 256
V7X_BF16_SUBLANES = 16
V7X_VMEM_BYTES = 64 * 1024 * 1024
HALO = V7X_BF16_SUBLANES
ROW_TILE = 256
PV_ROWS = HEAD_DIM + V7X_BF16_SUBLANES
KEY_TILE = V7X_MXU_COLS
SCORE_LOOKAHEAD = 2
ATTN_SUBTILES = 2
MOD_ROWS = 16
MOD_COLS = 512

BF16 = jnp.bfloat16
F32 = jnp.float32


def _dot(a, b):
    return jnp.dot(a, b, preferred_element_type=F32)


def _silu(x):
    return x * jax.nn.sigmoid(x)


def _modulation_kernel(cond_ref, w_ref, b_ref, o_ref):
    a = _silu(cond_ref[...]).astype(BF16)
    o_ref[0] = _dot(a, w_ref[0].astype(BF16)) + b_ref[0]


def _modulation(cond, w_ada, b_ada):
    depth = w_ada.shape[0]
    n_out = w_ada.shape[2]
    return pl.pallas_call(
        _modulation_kernel,
        grid=(depth, n_out // MOD_COLS),
        in_specs=[
            pl.BlockSpec((MOD_ROWS, D_MODEL), lambda l, j: (0, 0)),
            pl.BlockSpec((1, D_MODEL, MOD_COLS), lambda l, j: (l, 0, j)),
            pl.BlockSpec((1, 1, MOD_COLS), lambda l, j: (l, 0, j)),
        ],
        out_specs=pl.BlockSpec((1, MOD_ROWS, MOD_COLS), lambda l, j: (l, 0, j)),
        out_shape=jax.ShapeDtypeStruct((depth, MOD_ROWS, n_out), F32),
        name="modulation",
    )(cond, w_ada, b_ada.reshape(depth, 1, n_out))


def _head_norm_rope(raw, gain, ones_ref, rope_refs, row0, tm, out_scale):
    sumsq = _dot((raw * raw).astype(BF16), ones_ref[...])
    r = lax.rsqrt(sumsq * (1.0 / HEAD_DIM) + EPS)
    a = raw * gain
    if rope_refs is None:
        return a * (r * out_scale) if out_scale != 1.0 else a * r
    cos_ref, sin_lo_ref, sin_hi_ref = rope_refs
    rows = pl.ds(row0, tm)
    cos, sin_lo, sin_hi = cos_ref[rows, :], sin_lo_ref[rows, :], sin_hi_ref[rows, :]
    parts = []
    for m in range(V7X_MXU_COLS // V7X_LANES):
        c = a[:, m * V7X_LANES:(m + 1) * V7X_LANES]
        up = pltpu.roll(c, V7X_LANES - HEAD_DIM // 4, 1)
        down = pltpu.roll(c, HEAD_DIM // 4, 1)
        parts.append(c * cos + up * sin_lo + down * sin_hi)
    rot = jnp.concatenate(parts, axis=1)
    return rot * (r * out_scale) if out_scale != 1.0 else rot * r


def _projection_kernel(*refs, rope, full, tm, n_tiles):
    if full:
        (x_ref, xp_ref, xn_ref, shift_ref, scale_ref, ng_ref, w_ref, qg_ref, kg_ref, ones_ref,
         cos_ref, sin_lo_ref, sin_hi_ref, cw_ref, cb_ref, wcb_ref, bg_ref,
         qt_out, k_out, vt_out, sza_out, g0_out, t1_out, h_scr, u_scr, yz_scr) = refs
    else:
        (x_ref, shift_ref, scale_ref, ng_ref, w_ref, kg_ref, ones_ref,
         cos_ref, sin_lo_ref, sin_hi_ref, k_out, vt_out, h_scr) = refs
    i = pl.program_id(1)
    rope_refs = (cos_ref, sin_lo_ref, sin_hi_ref) if rope else None
    row0 = pl.multiple_of(i * tm, tm)

    mod_gain = ng_ref[...] * (1.0 + scale_ref[0])
    mod_shift = shift_ref[0]

    def modulated(xv):
        ms = jnp.mean(xv * xv, axis=-1, keepdims=True)
        return ((xv * lax.rsqrt(ms + EPS)) * mod_gain + mod_shift).astype(BF16)

    if full:
        h_scr[0:HALO, :] = modulated(xp_ref[0])
        h_scr[HALO:HALO + tm, :] = modulated(x_ref[0])
        h_scr[HALO + tm:, :] = modulated(xn_ref[0])
        h_main = h_scr[HALO:HALO + tm, :]
    else:
        h_scr[...] = modulated(x_ref[0])
        h_main = h_scr[...]

    def w_cols(off, g):
        return w_ref[:, off + g * V7X_MXU_COLS: off + (g + 1) * V7X_MXU_COLS]

    def group(off, g):
        return _dot(h_main, w_cols(off, g))

    k_w = w_cols(OFF_K, 0) if full else w_ref[:, 0:KV_WIDTH]
    v_w = w_cols(OFF_V, 0) if full else w_ref[:, KV_WIDTH:2 * KV_WIDTH]
    def finish_keys(k_raw):
        k = _head_norm_rope(k_raw, kg_ref[...], ones_ref, rope_refs, row0, tm, 1.0)
        for m in range(KV_WIDTH // V7X_LANES):
            k_pair = k[:, m * V7X_LANES:(m + 1) * V7X_LANES]
            k_out[0, 2 * m] = k_pair[:, :HEAD_DIM].astype(BF16)
            k_out[0, 2 * m + 1] = pltpu.roll(k_pair, HEAD_DIM, 1)[:, :HEAD_DIM].astype(BF16)

    def finish_queries(g, q_raw):
        q = _head_norm_rope(q_raw, qg_ref[...], ones_ref, rope_refs, row0, tm, SCORE_SCALE)
        for m in range(V7X_MXU_COLS // V7X_LANES):
            r0 = g * V7X_MXU_COLS + m * V7X_LANES
            qt_out[0, r0:r0 + V7X_LANES, :] = q[:, m * V7X_LANES:(m + 1) * V7X_LANES].T.astype(BF16)

    k_raw = _dot(h_main, k_w)
    v = _dot(h_main, v_w)
    for m in range(KV_WIDTH // V7X_LANES):
        vt_pair = v[:, m * V7X_LANES:(m + 1) * V7X_LANES].T.astype(BF16)
        vt_out[0, 2 * m, 0:HEAD_DIM, :] = vt_pair[:HEAD_DIM]
        vt_out[0, 2 * m + 1, 0:HEAD_DIM, :] = vt_pair[HEAD_DIM:]
    vt_out[0, :, HEAD_DIM:, :] = jnp.ones((N_KV_HEADS, PV_ROWS - HEAD_DIM, tm), BF16)
    if not full:
        finish_keys(k_raw)
        return

    finish_previous = functools.partial(finish_keys, k_raw)
    for g in range(ATTN_WIDTH // V7X_MXU_COLS):
        cols = slice(g * V7X_MXU_COLS, (g + 1) * V7X_MXU_COLS)
        q_raw = group(OFF_Q, g)
        finish_previous()
        finish_previous = functools.partial(finish_queries, g, q_raw)
        sza_out[0, :, cols] = _silu(group(OFF_ZA, g)).astype(BF16)
    finish_previous()

    h_ext = h_scr[...]
    row = lax.broadcasted_iota(jnp.int32, (tm, V7X_MXU_COLS), 0)
    first_row = jnp.logical_and(row == 0, i == 0)
    last_row = jnp.logical_and(row == tm - 1, i == n_tiles - 1)
    for g in range(CONV_WIDTH // V7X_MXU_COLS):
        cols = slice(g * V7X_MXU_COLS, (g + 1) * V7X_MXU_COLS)
        u_scr[...] = _dot(h_ext, w_cols(OFF_CC, g)) * _dot(h_ext, w_cols(OFF_XC, g))
        u_prev = jnp.where(first_row, 0.0, u_scr[HALO - 1:HALO - 1 + tm, :])
        u_next = jnp.where(last_row, 0.0, u_scr[HALO + 1:HALO + 1 + tm, :])
        conv = (u_prev * cw_ref[0:1, cols] + u_scr[HALO:HALO + tm, :] * cw_ref[1:2, cols]
                + u_next * cw_ref[2:3, cols] + cb_ref[:, cols])
        y = group(OFF_BC, g) * conv
        yz_scr[:, cols] = (y * _silu(group(OFF_ZC, g))).astype(BF16)

    yz = yz_scr[...]
    for g in range(D_MODEL // V7X_MXU_COLS):
        cols = slice(g * V7X_MXU_COLS, (g + 1) * V7X_MXU_COLS)
        g0_out[0, :, cols] = jax.nn.sigmoid(group(OFF_GL, g) + bg_ref[0:1, cols]).astype(BF16)
        gate1 = jax.nn.sigmoid(group(OFF_GL + D_MODEL, g) + bg_ref[1:2, cols])
        t1_out[0, :, cols] = gate1 * _dot(yz, wcb_ref[:, cols])


def _const_spec(shape):
    return pl.BlockSpec(shape, lambda b, i: (0,) * len(shape), pipeline_mode=pl.Buffered(1))


def _layer_spec(shape, layer, col_block=0):
    zeros = (0,) * (len(shape) - 1)
    return pl.BlockSpec((None, *shape), lambda b, i: (layer, *zeros, col_block), pipeline_mode=pl.Buffered(1))


def _projection(x, mod, mod_row, layer, p, rope_tabs, *, rope, full):
    bsz, seq, _ = x.shape
    tm = min(2 * ROW_TILE, seq)
    n_tiles = seq // tm
    halo_per_tile = tm // HALO
    n_halo = seq // HALO
    mod_base = layer * MOD_ROWS

    def tile_specs(tile_of):
        def mod_spec(part):
            if mod_row is None:
                return pl.BlockSpec((1, 1, D_MODEL), lambda b, i: (mod_base + tile_of(b, i)[0], 0, part))
            return pl.BlockSpec((1, 1, D_MODEL), lambda b, i: (mod_base + mod_row, 0, part))

        def prev_halo(b, i):
            tb, ti = tile_of(b, i)
            return tb, jnp.maximum(ti * halo_per_tile - 1, 0), 0

        def next_halo(b, i):
            tb, ti = tile_of(b, i)
            return tb, jnp.minimum((ti + 1) * halo_per_tile, n_halo - 1), 0

        return [pl.BlockSpec((1, tm, D_MODEL), lambda b, i: (*tile_of(b, i), 0)),
                pl.BlockSpec((1, HALO, D_MODEL), prev_halo), pl.BlockSpec((1, HALO, D_MODEL), next_halo),
                mod_spec(0), mod_spec(1)]


    tab_spec = _const_spec(rope_tabs[0].shape)
    k_spec = pl.BlockSpec((1, N_KV_HEADS, tm, HEAD_DIM), lambda b, i: (b, 0, i, 0))
    vt_spec = pl.BlockSpec((1, N_KV_HEADS, PV_ROWS, tm), lambda b, i: (b, 0, 0, i))
    k_shape = jax.ShapeDtypeStruct((bsz, N_KV_HEADS, seq, HEAD_DIM), BF16)
    vt_shape = jax.ShapeDtypeStruct((bsz, N_KV_HEADS, PV_ROWS, seq), BF16)
    qt_spec = pl.BlockSpec((1, ATTN_WIDTH, tm), lambda b, i: (b, 0, i))
    row_spec = pl.BlockSpec((1, tm, D_MODEL), lambda b, i: (b, i, 0))
    vmem_limit = V7X_VMEM_BYTES * 7 // 8
    gain_spec = _layer_spec((1, V7X_MXU_COLS), layer)

    kern = functools.partial(_projection_kernel, rope=rope, full=full, tm=tm, n_tiles=n_tiles)
    if full:
        in_specs = [*tile_specs(lambda b, i: (b, i)), _layer_spec((1, D_MODEL), layer),
                    _layer_spec((D_MODEL, PROJ_WIDTH), layer), gain_spec, gain_spec,
                    _const_spec(p["ones"].shape), tab_spec, tab_spec, tab_spec,
                    _layer_spec(p["conv_w"].shape[1:], layer), _layer_spec((1, CONV_WIDTH), layer),
                    _layer_spec((CONV_WIDTH, D_MODEL), layer), _layer_spec(p["b_gate"].shape[1:], layer)]
        args = (x, x, x, mod, mod, p["norm_g"], p["w_in"], p["q_gain"], p["k_gain"], p["ones"], *rope_tabs,
                p["conv_w"], p["conv_b"], p["w_conv_br"], p["b_gate"])
        out_specs = [qt_spec, k_spec, vt_spec, row_spec, row_spec, row_spec]
        out_shape = [jax.ShapeDtypeStruct((bsz, ATTN_WIDTH, seq), BF16), k_shape, vt_shape,
                     jax.ShapeDtypeStruct((bsz, seq, ATTN_WIDTH), BF16),
                     jax.ShapeDtypeStruct((bsz, seq, D_MODEL), BF16),
                     jax.ShapeDtypeStruct((bsz, seq, D_MODEL), F32)]
        scratch = [pltpu.VMEM((tm + 2 * HALO, D_MODEL), BF16),
                   pltpu.VMEM((tm + 2 * HALO, V7X_MXU_COLS), F32),
                   pltpu.VMEM((tm, CONV_WIDTH), BF16)]
    else:
        kv_cols = 2 * KV_WIDTH
        x_spec, _, _, shift_spec, scale_spec = tile_specs(lambda b, i: (b, i))
        in_specs = [x_spec, shift_spec, scale_spec, _layer_spec((1, D_MODEL), layer),
                    _layer_spec((D_MODEL, kv_cols), layer, col_block=OFF_K // kv_cols),
                    gain_spec, _const_spec(p["ones"].shape), tab_spec, tab_spec, tab_spec]
        args = (x, mod, mod, p["norm_g"], p["w_in"], p["k_gain"], p["ones"], *rope_tabs)
        out_specs = [k_spec, vt_spec]
        out_shape = [k_shape, vt_shape]
        scratch = [pltpu.VMEM((tm, D_MODEL), BF16)]
    return pl.pallas_call(
        kern,
        grid=(bsz, n_tiles),
        in_specs=in_specs,
        out_specs=out_specs,
        out_shape=out_shape,
        scratch_shapes=scratch,
        compiler_params=pltpu.CompilerParams(dimension_semantics=("arbitrary", "arbitrary"),
                                             vmem_limit_bytes=vmem_limit),
        name="projection_full" if full else "projection_kv",
    )(*args)


def _attention_kernel(*refs, n_seg):
    slot0_ref, qt_ref = refs[0], refs[1]
    refs = refs[1:]
    k_refs, vt_refs = refs[1:1 + n_seg], refs[1 + n_seg:1 + 2 * n_seg]
    (sza_ref, g0_ref, t1_ref, x_ref, gate_ref, wab_ref, wout_ref, o_ref, attn_scr, st_scr) = refs[1 + 2 * n_seg:]
    slot0 = slot0_ref[0]
    tiles, ring0 = [], 0
    for seg in range(n_seg):
        seg_keys = k_refs[seg].shape[2]
        size = min(KEY_TILE, seg_keys)
        for start in range(0, seg_keys, size):
            tiles.append((seg, slice(start, start + size), slice(ring0 + start, ring0 + start + size)))
        ring0 += seg_keys
    n_tiles = len(tiles)
    group = N_Q_HEADS // N_KV_HEADS
    sublanes = 8
    slots = st_scr.shape[0]
    tq = st_scr.shape[2]
    units = [(sub, h) for sub in range(qt_ref.shape[2] // tq) for h in range(N_Q_HEADS)]

    def score_tile(g, t, m8):
        sub, h = units[g]
        seg, seg_rows, ring_rows = tiles[t]
        s = _dot(k_refs[seg][0, h // group, seg_rows, :],
                 qt_ref[0, h * HEAD_DIM:(h + 1) * HEAD_DIM, sub * tq:(sub + 1) * tq])
        st_scr[slot0 + g % slots, ring_rows, :] = s.astype(BF16)
        for r in range(s.shape[0] // sublanes):
            piece = s[r * sublanes:(r + 1) * sublanes]
            m8 = piece if m8 is None else jnp.maximum(m8, piece)
        return m8

    def pv_tile(g, t, m_row, acc):
        seg, seg_rows, ring_rows = tiles[t]
        kv = units[g][1] // group
        p = jnp.exp2(st_scr[slot0 + g % slots, ring_rows, :] - m_row)
        part = _dot(vt_refs[seg][0, kv, :, seg_rows], p)
        return part if acc is None else acc + part

    def merge_stages(sub):
        rows = slice(sub * tq, (sub + 1) * tq)
        state = {}

        def branch():
            state["br"] = _dot(attn_scr[sub], wab_ref[...])

        def project():
            merged = g0_ref[0, rows, :].astype(F32) * state.pop("br") + t1_ref[0, rows, :]
            state["out"] = _dot(merged.astype(BF16), wout_ref[...])

        def residual():
            o_ref[0, rows, :] = x_ref[0, rows, :] + gate_ref[0] * state.pop("out")

        return [branch, project, residual]

    col_max = {}
    for g in range(min(SCORE_LOOKAHEAD, len(units))):
        m8 = None
        for t in range(n_tiles):
            m8 = score_tile(g, t, m8)
        col_max[g] = jnp.max(m8, axis=0, keepdims=True).astype(BF16)
    halves, pending = [], []
    for g, (sub, h) in enumerate(units):
        if pending:
            pending.pop(0)()
        acc, m8 = None, None
        ahead = g + SCORE_LOOKAHEAD
        m_row = col_max.pop(g)
        for t in range(n_tiles):
            if ahead < len(units):
                m8 = score_tile(ahead, t, m8)
            acc = pv_tile(g, t, m_row, acc)
        if ahead < len(units):
            col_max[ahead] = jnp.max(m8, axis=0, keepdims=True).astype(BF16)
        halves.append(acc[:HEAD_DIM] / acc[HEAD_DIM:HEAD_DIM + 1])
        if len(halves) == V7X_LANES // HEAD_DIM:
            j = h // len(halves)
            lanes = slice(j * V7X_LANES, (j + 1) * V7X_LANES)
            attn = jnp.concatenate(halves, axis=0).T
            gate = sza_ref[0, sub * tq:(sub + 1) * tq, lanes].astype(F32)
            attn_scr[sub, :, lanes] = (attn * gate).astype(BF16)
            halves = []
        if h == N_Q_HEADS - 1:
            pending += merge_stages(sub)
    for stage in pending:
        stage()


def _attention(qt, ks, vts, sza, g0, t1, x, mod, mod_row, layer, w_attn_br, w_out):
    bsz, seq, _ = x.shape
    sub_rows = min(ROW_TILE, seq)
    tq = min(ATTN_SUBTILES * sub_rows, seq)
    n_keys = sum(k.shape[2] for k in ks)
    mod_base = layer * MOD_ROWS
    k_specs = [pl.BlockSpec((1, N_KV_HEADS, k.shape[2], HEAD_DIM), lambda b, i: (b, 0, 0, 0)) for k in ks]
    vt_specs = [pl.BlockSpec((1, N_KV_HEADS, PV_ROWS, vt.shape[3]), lambda b, i: (b, 0, 0, 0)) for vt in vts]
    row_spec = pl.BlockSpec((1, tq, D_MODEL), lambda b, i: (b, i, 0))
    if mod_row is None:
        gate_spec = pl.BlockSpec((1, 1, D_MODEL), lambda b, i: (mod_base + b, 0, 2))
    else:
        gate_spec = pl.BlockSpec((1, 1, D_MODEL), lambda b, i: (mod_base + mod_row, 0, 2))
    return pl.pallas_call(
        functools.partial(_attention_kernel, n_seg=len(ks)),
        grid=(bsz, seq // tq),
        in_specs=[pl.BlockSpec(memory_space=pltpu.SMEM),
                  pl.BlockSpec((1, ATTN_WIDTH, tq), lambda b, i: (b, 0, i)), *k_specs, *vt_specs,
                  row_spec, row_spec, row_spec, row_spec, gate_spec,
                  _layer_spec(w_attn_br.shape[1:], layer), _layer_spec(w_out.shape[1:], layer)],
        out_specs=row_spec,
        out_shape=jax.ShapeDtypeStruct(x.shape, F32),
        scratch_shapes=[pltpu.VMEM((tq // sub_rows, sub_rows, ATTN_WIDTH), BF16),
                        pltpu.VMEM((SCORE_LOOKAHEAD + 1, n_keys, sub_rows), BF16)],
        compiler_params=pltpu.CompilerParams(dimension_semantics=("arbitrary", "arbitrary"),
                                             vmem_limit_bytes=V7X_VMEM_BYTES * 3 // 4),
        name="attention_merge",
    )(jnp.zeros((1,), jnp.int32), qt, *ks, *vts, sza, g0, t1, x, mod, w_attn_br, w_out)


def _rope_tables(n_tokens):
    t = np.arange(n_tokens)
    row = (t // GRID_W).astype(np.float32)
    col = (t % GRID_W).astype(np.float32)
    half = HEAD_DIM // 2
    inv = (np.float32(ROPE_THETA) ** (-np.arange(0, half, 2, dtype=np.float32) / np.float32(half))).astype(np.float32)
    ang_r = row[:, None] * inv[None, :]
    ang_c = col[:, None] * inv[None, :]
    ang = np.concatenate([ang_r, ang_r, ang_c, ang_c], axis=-1).astype(np.float64)
    cos, sin = np.cos(ang), np.sin(ang)
    first = (np.arange(HEAD_DIM) % half) < half // 2
    sin_lo = np.where(first[None, :], -sin, 0.0)
    sin_hi = np.where(first[None, :], 0.0, sin)
    reps = V7X_LANES // HEAD_DIM
    return tuple(jnp.asarray(np.tile(a, (1, reps)), dtype=F32) for a in (cos, sin_lo, sin_hi))


def _identity_rope_tables(n_tokens):
    one = jnp.ones((n_tokens, V7X_LANES), F32)
    zero = jnp.zeros((n_tokens, V7X_LANES), F32)
    return one, zero, zero


def _head_sum_matrix():
    idx = np.arange(V7X_MXU_COLS) // HEAD_DIM
    return jnp.asarray(idx[:, None] == idx[None, :], dtype=BF16)


def kernel(x, c, ctx, c_ctx, norm_g, w_ada, b_ada, w_in, q_norm_g, k_norm_g, conv_w, conv_b, w_attn_br, w_conv_br,
           b_gate, w_out):
    depth = w_ada.shape[0]
    bsz, seq, _ = x.shape
    ctx_len = ctx.shape[1]
    assert bsz + 1 <= MOD_ROWS and seq % ROW_TILE == 0 and ctx_len % HALO == 0 and ctx_len <= ROW_TILE

    cond = jnp.concatenate([c, c_ctx[None, :], jnp.zeros((MOD_ROWS - bsz - 1, D_MODEL), F32)], axis=0)
    mod = _modulation(cond, w_ada, b_ada).reshape(depth * MOD_ROWS, 1, 3 * D_MODEL)
    ctx_row = bsz

    rope_x = _rope_tables(seq)
    rope_c = _identity_rope_tables(ctx_len)
    reps = V7X_MXU_COLS // HEAD_DIM
    p = dict(
        norm_g=norm_g.reshape(depth, 1, D_MODEL),
        w_in=w_in.astype(BF16),
        q_gain=jnp.tile(q_norm_g, (1, reps)).reshape(depth, 1, V7X_MXU_COLS),
        k_gain=jnp.tile(k_norm_g, (1, reps)).reshape(depth, 1, V7X_MXU_COLS),
        ones=_head_sum_matrix(),
        conv_w=conv_w,
        conv_b=conv_b.reshape(depth, 1, CONV_WIDTH),
        w_conv_br=w_conv_br.astype(BF16),
        b_gate=b_gate,
    )
    wab = w_attn_br.astype(BF16)
    wo = w_out.astype(BF16)

    for l in range(depth):
        last = l == depth - 1
        qx, ktx, vx, szax, g0x, t1x = _projection(x, mod, None, l, p, rope_x, rope=True, full=True)
        if last:
            ktc, vc = _projection(ctx, mod, ctx_row, l, p, rope_c, rope=False, full=False)
        else:
            qc, ktc, vc, szac, g0c, t1c = _projection(ctx, mod, ctx_row, l, p, rope_c, rope=False, full=True)
        x_new = _attention(qx, [ktc, ktx], [vc, vx], szax, g0x, t1x, x, mod, None, l, wab, wo)
        if not last:
            ctx = _attention(qc, [ktc], [vc], szac, g0c, t1c, ctx, mod, ctx_row, l, wab, wo)
        x = x_new
    return x
```

```python
import functools

import numpy as np
import jax
import jax.numpy as jnp
from jax import lax
from jax.experimental import pallas as pl
from jax.experimental.pallas import tpu as pltpu

D_MODEL = 1024
GRID_W = 64
HEAD_DIM = 64
N_Q_HEADS = 16
N_KV_HEADS = 4
ATTN_WIDTH = N_Q_HEADS * HEAD_DIM
KV_WIDTH = N_KV_HEADS * HEAD_DIM
CONV_WIDTH = D_MODEL
ROPE_THETA = 10000.0
EPS = 1e-6
ATTN_SCALE = HEAD_DIM ** -0.5
SCORE_SCALE = ATTN_SCALE * 1.4426950408889634

OFF_Q = 0
OFF_K = OFF_Q + ATTN_WIDTH
OFF_V = OFF_K + KV_WIDTH
OFF_ZA = OFF_V + KV_WIDTH
OFF_XC = OFF_ZA + ATTN_WIDTH
OFF_BC = OFF_XC + CONV_WIDTH
OFF_CC = OFF_BC + CONV_WIDTH
OFF_ZC = OFF_CC + CONV_WIDTH
OFF_GL = OFF_ZC + CONV_WIDTH
PROJ_WIDTH = OFF_GL + 2 * D_MODEL

V7X_LANES = 128
V7X_MXU_COLS = 256
V7X_BF16_SUBLANES = 16
V7X_VMEM_BYTES = 64 * 1024 * 1024
HALO = V7X_BF16_SUBLANES
PROJ_ROW_TILE = 512
QUERY_SUBTILE = V7X_MXU_COLS
ATTN_SUBTILES = 2
PV_ROWS = HEAD_DIM + V7X_BF16_SUBLANES
KEY_TILE = V7X_MXU_COLS
SCORE_LOOKAHEAD = 2
MOD_ROWS = 16
MOD_COLS = 1024
PROJ_VMEM_LIMIT = V7X_VMEM_BYTES * 7 // 8
ATTN_VMEM_LIMIT = V7X_VMEM_BYTES * 3 // 4

BF16 = jnp.bfloat16
F32 = jnp.float32


def _dot(a, b):
    return jnp.dot(a, b, preferred_element_type=F32)


def _silu(x):
    return x * jax.nn.sigmoid(x)


def _modulation_kernel(cond_ref, w_ref, b_ref, o_ref):
    a = _silu(cond_ref[...]).astype(BF16)
    o_ref[0] = _dot(a, w_ref[0].astype(BF16)) + b_ref[0]


def _modulation(cond, w_ada, b_ada):
    depth = w_ada.shape[0]
    n_out = w_ada.shape[2]
    return pl.pallas_call(
        _modulation_kernel,
        grid=(depth, n_out // MOD_COLS),
        in_specs=[
            pl.BlockSpec((MOD_ROWS, D_MODEL), lambda l, j: (0, 0)),
            pl.BlockSpec((1, D_MODEL, MOD_COLS), lambda l, j: (l, 0, j)),
            pl.BlockSpec((1, 1, MOD_COLS), lambda l, j: (l, 0, j)),
        ],
        out_specs=pl.BlockSpec((1, MOD_ROWS, MOD_COLS), lambda l, j: (l, 0, j)),
        out_shape=jax.ShapeDtypeStruct((depth, MOD_ROWS, n_out), F32),
        name="modulation",
    )(cond, w_ada, b_ada.reshape(depth, 1, n_out))


def _head_norm_rope(raw, gain, ones_ref, rope_refs, row0, tm, out_scale):
    sumsq = _dot((raw * raw).astype(BF16), ones_ref[...])
    r = lax.rsqrt(sumsq * (1.0 / HEAD_DIM) + EPS)
    a = raw * gain
    if rope_refs is None:
        return a * (r * out_scale) if out_scale != 1.0 else a * r
    cos_ref, sin_lo_ref, sin_hi_ref = rope_refs
    rows = pl.ds(row0, tm)
    cos, sin_lo, sin_hi = cos_ref[rows, :], sin_lo_ref[rows, :], sin_hi_ref[rows, :]
    parts = []
    for m in range(V7X_MXU_COLS // V7X_LANES):
        c = a[:, m * V7X_LANES:(m + 1) * V7X_LANES]
        up = pltpu.roll(c, V7X_LANES - HEAD_DIM // 4, 1)
        down = pltpu.roll(c, HEAD_DIM // 4, 1)
        parts.append(c * cos + up * sin_lo + down * sin_hi)
    rot = jnp.concatenate(parts, axis=1)
    return rot * (r * out_scale) if out_scale != 1.0 else rot * r


def _projection_kernel(*refs, rope, full, tm, n_tiles):
    if full:
        (x_ref, xp_ref, xn_ref, shift_ref, scale_ref, ng_ref, w_ref, qg_ref, kg_ref, ones_ref,
         cos_ref, sin_lo_ref, sin_hi_ref, cw_ref, cb_ref, wcb_ref, bg_ref,
         qt_out, k_out, vt_out, sza_out, g0_out, t1_out, h_scr, u_scr, yz_scr) = refs
    else:
        (x_ref, shift_ref, scale_ref, ng_ref, w_ref, kg_ref, ones_ref,
         cos_ref, sin_lo_ref, sin_hi_ref, k_out, vt_out, h_scr) = refs
    i = pl.program_id(1)
    rope_refs = (cos_ref, sin_lo_ref, sin_hi_ref) if rope else None
    row0 = pl.multiple_of(i * tm, tm)

    mod_gain = ng_ref[...] * (1.0 + scale_ref[0])
    mod_shift = shift_ref[0]

    def modulated(xv):
        ms = jnp.mean(xv * xv, axis=-1, keepdims=True)
        return ((xv * lax.rsqrt(ms + EPS)) * mod_gain + mod_shift).astype(BF16)

    if full:
        h_scr[0:HALO, :] = modulated(xp_ref[0])
        h_scr[HALO:HALO + tm, :] = modulated(x_ref[0])
        h_scr[HALO + tm:, :] = modulated(xn_ref[0])
        h_main = h_scr[HALO:HALO + tm, :]
    else:
        h_scr[...] = modulated(x_ref[0])
        h_main = h_scr[...]

    def w_cols(off, g):
        return w_ref[:, off + g * V7X_MXU_COLS: off + (g + 1) * V7X_MXU_COLS]

    def group(off, g):
        return _dot(h_main, w_cols(off, g))

    k_w = w_cols(OFF_K, 0) if full else w_ref[:, 0:KV_WIDTH]
    v_w = w_cols(OFF_V, 0) if full else w_ref[:, KV_WIDTH:2 * KV_WIDTH]
    def finish_keys(k_raw):
        k = _head_norm_rope(k_raw, kg_ref[...], ones_ref, rope_refs, row0, tm, 1.0)
        for m in range(KV_WIDTH // V7X_LANES):
            k_pair = k[:, m * V7X_LANES:(m + 1) * V7X_LANES]
            k_out[0, 2 * m] = k_pair[:, :HEAD_DIM].astype(BF16)
            k_out[0, 2 * m + 1] = pltpu.roll(k_pair, HEAD_DIM, 1)[:, :HEAD_DIM].astype(BF16)

    def finish_queries(g, q_raw):
        q = _head_norm_rope(q_raw, qg_ref[...], ones_ref, rope_refs, row0, tm, SCORE_SCALE)
        for m in range(V7X_MXU_COLS // V7X_LANES):
            r0 = g * V7X_MXU_COLS + m * V7X_LANES
            qt_out[0, r0:r0 + V7X_LANES, :] = q[:, m * V7X_LANES:(m + 1) * V7X_LANES].T.astype(BF16)

    k_raw = _dot(h_main, k_w)
    v = _dot(h_main, v_w)
    for m in range(KV_WIDTH // V7X_LANES):
        vt_pair = v[:, m * V7X_LANES:(m + 1) * V7X_LANES].T.astype(BF16)
        vt_out[0, 2 * m, 0:HEAD_DIM, :] = vt_pair[:HEAD_DIM]
        vt_out[0, 2 * m + 1, 0:HEAD_DIM, :] = vt_pair[HEAD_DIM:]
    vt_out[0, :, HEAD_DIM:, :] = jnp.ones((N_KV_HEADS, PV_ROWS - HEAD_DIM, tm), BF16)
    if not full:
        finish_keys(k_raw)
        return

    finish_previous = functools.partial(finish_keys, k_raw)
    for g in range(ATTN_WIDTH // V7X_MXU_COLS):
        cols = slice(g * V7X_MXU_COLS, (g + 1) * V7X_MXU_COLS)
        q_raw = group(OFF_Q, g)
        finish_previous()
        finish_previous = functools.partial(finish_queries, g, q_raw)
        sza_out[0, :, cols] = _silu(group(OFF_ZA, g)).astype(BF16)
    finish_previous()

    h_ext = h_scr[...]
    row = lax.broadcasted_iota(jnp.int32, (tm, V7X_MXU_COLS), 0)
    first_row = jnp.logical_and(row == 0, i == 0)
    last_row = jnp.logical_and(row == tm - 1, i == n_tiles - 1)
    for g in range(CONV_WIDTH // V7X_MXU_COLS):
        cols = slice(g * V7X_MXU_COLS, (g + 1) * V7X_MXU_COLS)
        u_scr[...] = _dot(h_ext, w_cols(OFF_CC, g)) * _dot(h_ext, w_cols(OFF_XC, g))
        u_prev = jnp.where(first_row, 0.0, u_scr[HALO - 1:HALO - 1 + tm, :])
        u_next = jnp.where(last_row, 0.0, u_scr[HALO + 1:HALO + 1 + tm, :])
        conv = (u_prev * cw_ref[0:1, cols] + u_scr[HALO:HALO + tm, :] * cw_ref[1:2, cols]
                + u_next * cw_ref[2:3, cols] + cb_ref[:, cols])
        y = group(OFF_BC, g) * conv
        yz_scr[:, cols] = (y * _silu(group(OFF_ZC, g))).astype(BF16)

    yz = yz_scr[...]
    for g in range(D_MODEL // V7X_MXU_COLS):
        cols = slice(g * V7X_MXU_COLS, (g + 1) * V7X_MXU_COLS)
        g0_out[0, :, cols] = jax.nn.sigmoid(group(OFF_GL, g) + bg_ref[0:1, cols]).astype(BF16)
        gate1 = jax.nn.sigmoid(group(OFF_GL + D_MODEL, g) + bg_ref[1:2, cols])
        t1_out[0, :, cols] = gate1 * _dot(yz, wcb_ref[:, cols])


def _const_spec(shape):
    return pl.BlockSpec(shape, lambda b, i: (0,) * len(shape), pipeline_mode=pl.Buffered(1))


def _layer_spec(shape, layer, col_block=0):
    zeros = (0,) * (len(shape) - 1)
    return pl.BlockSpec((None, *shape), lambda b, i: (layer, *zeros, col_block), pipeline_mode=pl.Buffered(1))


def _projection(x, mod, mod_row, layer, p, rope_tabs, *, rope, full):
    bsz, seq, _ = x.shape
    tm = min(PROJ_ROW_TILE, seq)
    n_tiles = seq // tm
    halo_per_tile = tm // HALO
    n_halo = seq // HALO
    mod_base = layer * MOD_ROWS

    def tile_specs(tile_of):
        def mod_spec(part):
            if mod_row is None:
                return pl.BlockSpec((1, 1, D_MODEL), lambda b, i: (mod_base + tile_of(b, i)[0], 0, part))
            return pl.BlockSpec((1, 1, D_MODEL), lambda b, i: (mod_base + mod_row, 0, part))

        def prev_halo(b, i):
            tb, ti = tile_of(b, i)
            return tb, jnp.maximum(ti * halo_per_tile - 1, 0), 0

        def next_halo(b, i):
            tb, ti = tile_of(b, i)
            return tb, jnp.minimum((ti + 1) * halo_per_tile, n_halo - 1), 0

        return [pl.BlockSpec((1, tm, D_MODEL), lambda b, i: (*tile_of(b, i), 0)),
                pl.BlockSpec((1, HALO, D_MODEL), prev_halo), pl.BlockSpec((1, HALO, D_MODEL), next_halo),
                mod_spec(0), mod_spec(1)]


    tab_spec = _const_spec(rope_tabs[0].shape)
    k_spec = pl.BlockSpec((1, N_KV_HEADS, tm, HEAD_DIM), lambda b, i: (b, 0, i, 0))
    vt_spec = pl.BlockSpec((1, N_KV_HEADS, PV_ROWS, tm), lambda b, i: (b, 0, 0, i))
    k_shape = jax.ShapeDtypeStruct((bsz, N_KV_HEADS, seq, HEAD_DIM), BF16)
    vt_shape = jax.ShapeDtypeStruct((bsz, N_KV_HEADS, PV_ROWS, seq), BF16)
    qt_spec = pl.BlockSpec((1, ATTN_WIDTH, tm), lambda b, i: (b, 0, i))
    row_spec = pl.BlockSpec((1, tm, D_MODEL), lambda b, i: (b, i, 0))
    gain_spec = _layer_spec((1, V7X_MXU_COLS), layer)

    kern = functools.partial(_projection_kernel, rope=rope, full=full, tm=tm, n_tiles=n_tiles)
    if full:
        in_specs = [*tile_specs(lambda b, i: (b, i)), _layer_spec((1, D_MODEL), layer),
                    _layer_spec((D_MODEL, PROJ_WIDTH), layer), gain_spec, gain_spec,
                    _const_spec(p["ones"].shape), tab_spec, tab_spec, tab_spec,
                    _layer_spec(p["conv_w"].shape[1:], layer), _layer_spec((1, CONV_WIDTH), layer),
                    _layer_spec((CONV_WIDTH, D_MODEL), layer), _layer_spec(p["b_gate"].shape[1:], layer)]
        args = (x, x, x, mod, mod, p["norm_g"], p["w_in"], p["q_gain"], p["k_gain"], p["ones"], *rope_tabs,
                p["conv_w"], p["conv_b"], p["w_conv_br"], p["b_gate"])
        out_specs = [qt_spec, k_spec, vt_spec, row_spec, row_spec, row_spec]
        out_shape = [jax.ShapeDtypeStruct((bsz, ATTN_WIDTH, seq), BF16), k_shape, vt_shape,
                     jax.ShapeDtypeStruct((bsz, seq, ATTN_WIDTH), BF16),
                     jax.ShapeDtypeStruct((bsz, seq, D_MODEL), BF16),
                     jax.ShapeDtypeStruct((bsz, seq, D_MODEL), F32)]
        scratch = [pltpu.VMEM((tm + 2 * HALO, D_MODEL), BF16),
                   pltpu.VMEM((tm + 2 * HALO, V7X_MXU_COLS), F32),
                   pltpu.VMEM((tm, CONV_WIDTH), BF16)]
    else:
        kv_cols = 2 * KV_WIDTH
        x_spec, _, _, shift_spec, scale_spec = tile_specs(lambda b, i: (b, i))
        in_specs = [x_spec, shift_spec, scale_spec, _layer_spec((1, D_MODEL), layer),
                    _layer_spec((D_MODEL, kv_cols), layer, col_block=OFF_K // kv_cols),
                    gain_spec, _const_spec(p["ones"].shape), tab_spec, tab_spec, tab_spec]
        args = (x, mod, mod, p["norm_g"], p["w_in"], p["k_gain"], p["ones"], *rope_tabs)
        out_specs = [k_spec, vt_spec]
        out_shape = [k_shape, vt_shape]
        scratch = [pltpu.VMEM((tm, D_MODEL), BF16)]
    return pl.pallas_call(
        kern,
        grid=(bsz, n_tiles),
        in_specs=in_specs,
        out_specs=out_specs,
        out_shape=out_shape,
        scratch_shapes=scratch,
        compiler_params=pltpu.CompilerParams(dimension_semantics=("arbitrary", "arbitrary"),
                                             vmem_limit_bytes=PROJ_VMEM_LIMIT),
        name="projection_full" if full else "projection_kv",
    )(*args)


def _attention_kernel(*refs, n_seg):
    slot0_ref, qt_ref = refs[0], refs[1]
    refs = refs[1:]
    k_refs, vt_refs = refs[1:1 + n_seg], refs[1 + n_seg:1 + 2 * n_seg]
    (sza_ref, g0_ref, t1_ref, x_ref, gate_ref, wab_ref, wout_ref, o_ref, attn_scr, st_scr) = refs[1 + 2 * n_seg:]
    slot0 = slot0_ref[0]
    tiles, ring0 = [], 0
    for seg in range(n_seg):
        seg_keys = k_refs[seg].shape[2]
        size = min(KEY_TILE, seg_keys)
        for start in range(0, seg_keys, size):
            tiles.append((seg, slice(start, start + size), slice(ring0 + start, ring0 + start + size)))
        ring0 += seg_keys
    n_tiles = len(tiles)
    group = N_Q_HEADS // N_KV_HEADS
    sublanes = 8
    slots = st_scr.shape[0]
    tq = st_scr.shape[2]
    units = [(sub, h) for sub in range(qt_ref.shape[2] // tq) for h in range(N_Q_HEADS)]

    def score_tile(g, t, m8):
        sub, h = units[g]
        seg, seg_rows, ring_rows = tiles[t]
        s = _dot(k_refs[seg][0, h // group, seg_rows, :],
                 qt_ref[0, h * HEAD_DIM:(h + 1) * HEAD_DIM, sub * tq:(sub + 1) * tq])
        st_scr[slot0 + g % slots, ring_rows, :] = s.astype(BF16)
        for r in range(s.shape[0] // sublanes):
            piece = s[r * sublanes:(r + 1) * sublanes]
            m8 = piece if m8 is None else jnp.maximum(m8, piece)
        return m8

    def pv_tile(g, t, m_row, acc):
        seg, seg_rows, ring_rows = tiles[t]
        kv = units[g][1] // group
        p = jnp.exp2(st_scr[slot0 + g % slots, ring_rows, :] - m_row)
        part = _dot(vt_refs[seg][0, kv, :, seg_rows], p)
        return part if acc is None else acc + part

    def merge_stages(sub):
        rows = slice(sub * tq, (sub + 1) * tq)
        state = {}

        def branch():
            state["br"] = _dot(attn_scr[sub], wab_ref[...])

        def project():
            merged = g0_ref[0, rows, :].astype(F32) * state.pop("br") + t1_ref[0, rows, :]
            state["out"] = _dot(merged.astype(BF16), wout_ref[...])

        def residual():
            o_ref[0, rows, :] = x_ref[0, rows, :] + gate_ref[0] * state.pop("out")

        return [branch, project, residual]

    col_max = {}
    for g in range(min(SCORE_LOOKAHEAD, len(units))):
        m8 = None
        for t in range(n_tiles):
            m8 = score_tile(g, t, m8)
        col_max[g] = jnp.max(m8, axis=0, keepdims=True).astype(BF16)
    halves, pending = [], []
    for g, (sub, h) in enumerate(units):
        if pending:
            pending.pop(0)()
        acc, m8 = None, None
        ahead = g + SCORE_LOOKAHEAD
        m_row = col_max.pop(g)
        for t in range(n_tiles):
            if ahead < len(units):
                m8 = score_tile(ahead, t, m8)
            acc = pv_tile(g, t, m_row, acc)
        if ahead < len(units):
            col_max[ahead] = jnp.max(m8, axis=0, keepdims=True).astype(BF16)
        halves.append(acc[:HEAD_DIM] / acc[HEAD_DIM:HEAD_DIM + 1])
        if len(halves) == V7X_LANES // HEAD_DIM:
            j = h // len(halves)
            lanes = slice(j * V7X_LANES, (j + 1) * V7X_LANES)
            attn = jnp.concatenate(halves, axis=0).T
            gate = sza_ref[0, sub * tq:(sub + 1) * tq, lanes].astype(F32)
            attn_scr[sub, :, lanes] = (attn * gate).astype(BF16)
            halves = []
        if h == N_Q_HEADS - 1:
            pending += merge_stages(sub)
    for stage in pending:
        stage()


def _attention(qt, ks, vts, sza, g0, t1, x, mod, mod_row, layer, w_attn_br, w_out):
    bsz, seq, _ = x.shape
    sub_rows = min(QUERY_SUBTILE, seq)
    tq = min(ATTN_SUBTILES * sub_rows, seq)
    n_keys = sum(k.shape[2] for k in ks)
    mod_base = layer * MOD_ROWS
    k_specs = [pl.BlockSpec((1, N_KV_HEADS, k.shape[2], HEAD_DIM), lambda b, i: (b, 0, 0, 0)) for k in ks]
    vt_specs = [pl.BlockSpec((1, N_KV_HEADS, PV_ROWS, vt.shape[3]), lambda b, i: (b, 0, 0, 0)) for vt in vts]
    row_spec = pl.BlockSpec((1, tq, D_MODEL), lambda b, i: (b, i, 0))
    if mod_row is None:
        gate_spec = pl.BlockSpec((1, 1, D_MODEL), lambda b, i: (mod_base + b, 0, 2))
    else:
        gate_spec = pl.BlockSpec((1, 1, D_MODEL), lambda b, i: (mod_base + mod_row, 0, 2))
    return pl.pallas_call(
        functools.partial(_attention_kernel, n_seg=len(ks)),
        grid=(bsz, seq // tq),
        in_specs=[pl.BlockSpec(memory_space=pltpu.SMEM),
                  pl.BlockSpec((1, ATTN_WIDTH, tq), lambda b, i: (b, 0, i)), *k_specs, *vt_specs,
                  row_spec, row_spec, row_spec, row_spec, gate_spec,
                  _layer_spec(w_attn_br.shape[1:], layer), _layer_spec(w_out.shape[1:], layer)],
        out_specs=row_spec,
        out_shape=jax.ShapeDtypeStruct(x.shape, F32),
        scratch_shapes=[pltpu.VMEM((tq // sub_rows, sub_rows, ATTN_WIDTH), BF16),
                        pltpu.VMEM((SCORE_LOOKAHEAD + 1, n_keys, sub_rows), BF16)],
        compiler_params=pltpu.CompilerParams(dimension_semantics=("arbitrary", "arbitrary"),
                                             vmem_limit_bytes=ATTN_VMEM_LIMIT),
        name="attention_merge",
    )(jnp.zeros((1,), jnp.int32), qt, *ks, *vts, sza, g0, t1, x, mod, w_attn_br, w_out)


def _rope_tables(n_tokens):
    t = np.arange(n_tokens)
    row = (t // GRID_W).astype(np.float32)
    col = (t % GRID_W).astype(np.float32)
    half = HEAD_DIM // 2
    inv = (np.float32(ROPE_THETA) ** (-np.arange(0, half, 2, dtype=np.float32) / np.float32(half))).astype(np.float32)
    ang_r = row[:, None] * inv[None, :]
    ang_c = col[:, None] * inv[None, :]
    ang = np.concatenate([ang_r, ang_r, ang_c, ang_c], axis=-1).astype(np.float64)
    cos, sin = np.cos(ang), np.sin(ang)
    first = (np.arange(HEAD_DIM) % half) < half // 2
    sin_lo = np.where(first[None, :], -sin, 0.0)
    sin_hi = np.where(first[None, :], 0.0, sin)
    reps = V7X_LANES // HEAD_DIM
    return tuple(jnp.asarray(np.tile(a, (1, reps)), dtype=F32) for a in (cos, sin_lo, sin_hi))


def _identity_rope_tables(n_tokens):
    one = jnp.ones((n_tokens, V7X_LANES), F32)
    zero = jnp.zeros((n_tokens, V7X_LANES), F32)
    return one, zero, zero


def _head_sum_matrix():
    idx = np.arange(V7X_MXU_COLS) // HEAD_DIM
    return jnp.asarray(idx[:, None] == idx[None, :], dtype=BF16)


def kernel(x, c, ctx, c_ctx, norm_g, w_ada, b_ada, w_in, q_norm_g, k_norm_g, conv_w, conv_b, w_attn_br, w_conv_br,
           b_gate, w_out):
    depth = w_ada.shape[0]
    bsz, seq, _ = x.shape
    ctx_len = ctx.shape[1]
    assert bsz + 1 <= MOD_ROWS and w_ada.shape[2] % MOD_COLS == 0
    assert seq % PROJ_ROW_TILE == 0 and seq % (ATTN_SUBTILES * QUERY_SUBTILE) == 0 and seq % KEY_TILE == 0
    assert ctx_len <= min(PROJ_ROW_TILE, QUERY_SUBTILE, KEY_TILE) and ctx_len % V7X_LANES == 0

    cond = jnp.concatenate([c, c_ctx[None, :], jnp.zeros((MOD_ROWS - bsz - 1, D_MODEL), F32)], axis=0)
    mod = _modulation(cond, w_ada, b_ada).reshape(depth * MOD_ROWS, 1, 3 * D_MODEL)
    ctx_row = bsz

    rope_x = _rope_tables(seq)
    rope_c = _identity_rope_tables(ctx_len)
    reps = V7X_MXU_COLS // HEAD_DIM
    p = dict(
        norm_g=norm_g.reshape(depth, 1, D_MODEL),
        w_in=w_in.astype(BF16),
        q_gain=jnp.tile(q_norm_g, (1, reps)).reshape(depth, 1, V7X_MXU_COLS),
        k_gain=jnp.tile(k_norm_g, (1, reps)).reshape(depth, 1, V7X_MXU_COLS),
        ones=_head_sum_matrix(),
        conv_w=conv_w,
        conv_b=conv_b.reshape(depth, 1, CONV_WIDTH),
        w_conv_br=w_conv_br.astype(BF16),
        b_gate=b_gate,
    )
    wab = w_attn_br.astype(BF16)
    wo = w_out.astype(BF16)

    for l in range(depth):
        last = l == depth - 1
        qx, ktx, vx, szax, g0x, t1x = _projection(x, mod, None, l, p, rope_x, rope=True, full=True)
        if last:
            ktc, vc = _projection(ctx, mod, ctx_row, l, p, rope_c, rope=False, full=False)
        else:
            qc, ktc, vc, szac, g0c, t1c = _projection(ctx, mod, ctx_row, l, p, rope_c, rope=False, full=True)
        x_new = _attention(qx, [ktc, ktx], [vc, vx], szax, g0x, t1x, x, mod, None, l, wab, wo)
        if not last:
            ctx = _attention(qc, [ktc], [vc], szac, g0c, t1c, ctx, mod, ctx_row, l, wab, wo)
        x = x_new
    return x
```

```python
import functools

import numpy as np
import jax
import jax.numpy as jnp
from jax import lax
from jax.experimental import pallas as pl
from jax.experimental.pallas import tpu as pltpu

D_MODEL = 1024
GRID_W = 64
HEAD_DIM = 64
N_Q_HEADS = 16
N_KV_HEADS = 4
ATTN_WIDTH = N_Q_HEADS * HEAD_DIM
KV_WIDTH = N_KV_HEADS * HEAD_DIM
CONV_WIDTH = D_MODEL
ROPE_THETA = 10000.0
EPS = 1e-6
ATTN_SCALE = HEAD_DIM ** -0.5
SCORE_SCALE = ATTN_SCALE * 1.4426950408889634

OFF_Q = 0
OFF_K = OFF_Q + ATTN_WIDTH
OFF_V = OFF_K + KV_WIDTH
OFF_ZA = OFF_V + KV_WIDTH
OFF_XC = OFF_ZA + ATTN_WIDTH
OFF_BC = OFF_XC + CONV_WIDTH
OFF_CC = OFF_BC + CONV_WIDTH
OFF_ZC = OFF_CC + CONV_WIDTH
OFF_GL = OFF_ZC + CONV_WIDTH
PROJ_WIDTH = OFF_GL + 2 * D_MODEL

V7X_LANES = 128
V7X_MXU_COLS = 256
V7X_BF16_SUBLANES = 16
V7X_VMEM_BYTES = 64 * 1024 * 1024
HALO = V7X_BF16_SUBLANES
PROJ_ROW_TILE = 512
QUERY_SUBTILE = V7X_MXU_COLS
ATTN_SUBTILES = 2
PV_ROWS = HEAD_DIM + V7X_BF16_SUBLANES
KEY_TILE = V7X_MXU_COLS
SCORE_LOOKAHEAD = 2
MOD_ROWS = 16
MOD_COLS = 1024
PROJ_VMEM_LIMIT = V7X_VMEM_BYTES * 7 // 8
ATTN_VMEM_LIMIT = V7X_VMEM_BYTES * 3 // 4

BF16 = jnp.bfloat16
F32 = jnp.float32


def _dot(a, b):
    return jnp.dot(a, b, preferred_element_type=F32)


def _silu(x):
    return x * jax.nn.sigmoid(x)


def _modulation_kernel(cond_ref, w_ref, b_ref, o_ref):
    a = _silu(cond_ref[...]).astype(BF16)
    o_ref[0] = _dot(a, w_ref[0].astype(BF16)) + b_ref[0]


def _modulation(cond, w_ada, b_ada):
    depth = w_ada.shape[0]
    n_out = w_ada.shape[2]
    return pl.pallas_call(
        _modulation_kernel,
        grid=(depth, n_out // MOD_COLS),
        in_specs=[
            pl.BlockSpec((MOD_ROWS, D_MODEL), lambda l, j: (0, 0)),
            pl.BlockSpec((1, D_MODEL, MOD_COLS), lambda l, j: (l, 0, j)),
            pl.BlockSpec((1, 1, MOD_COLS), lambda l, j: (l, 0, j)),
        ],
        out_specs=pl.BlockSpec((1, MOD_ROWS, MOD_COLS), lambda l, j: (l, 0, j)),
        out_shape=jax.ShapeDtypeStruct((depth, MOD_ROWS, n_out), F32),
        name="modulation",
    )(cond, w_ada, b_ada.reshape(depth, 1, n_out))


def _head_norm_rope(raw, gain, ones_ref, rope_refs, row0, tm, out_scale):
    sumsq = _dot((raw * raw).astype(BF16), ones_ref[...])
    r = lax.rsqrt(sumsq * (1.0 / HEAD_DIM) + EPS)
    a = raw * gain
    if rope_refs is None:
        return a * (r * out_scale) if out_scale != 1.0 else a * r
    cos_ref, sin_lo_ref, sin_hi_ref = rope_refs
    rows = pl.ds(row0, tm)
    cos, sin_lo, sin_hi = cos_ref[rows, :], sin_lo_ref[rows, :], sin_hi_ref[rows, :]
    parts = []
    for m in range(V7X_MXU_COLS // V7X_LANES):
        c = a[:, m * V7X_LANES:(m + 1) * V7X_LANES]
        up = pltpu.roll(c, V7X_LANES - HEAD_DIM // 4, 1)
        down = pltpu.roll(c, HEAD_DIM // 4, 1)
        parts.append(c * cos + up * sin_lo + down * sin_hi)
    rot = jnp.concatenate(parts, axis=1)
    return rot * (r * out_scale) if out_scale != 1.0 else rot * r


def _projection_kernel(*refs, rope, full, tm, n_tiles):
    if full:
        (x_ref, xp_ref, xn_ref, shift_ref, scale_ref, ng_ref, w_ref, qg_ref, kg_ref, ones_ref,
         cos_ref, sin_lo_ref, sin_hi_ref, cw_ref, cb_ref, wcb_ref, bg_ref,
         qt_out, k_out, vt_out, sza_out, g0_out, t1_out, h_scr, u_scr, yz_scr) = refs
    else:
        (x_ref, shift_ref, scale_ref, ng_ref, w_ref, kg_ref, ones_ref,
         cos_ref, sin_lo_ref, sin_hi_ref, k_out, vt_out, h_scr) = refs
    i = pl.program_id(1)
    rope_refs = (cos_ref, sin_lo_ref, sin_hi_ref) if rope else None
    row0 = pl.multiple_of(i * tm, tm)

    mod_gain = ng_ref[...] * (1.0 + scale_ref[0])
    mod_shift = shift_ref[0]

    def modulated(xv):
        ms = jnp.mean(xv * xv, axis=-1, keepdims=True)
        return ((xv * lax.rsqrt(ms + EPS)) * mod_gain + mod_shift).astype(BF16)

    if full:
        h_scr[0:HALO, :] = modulated(xp_ref[0])
        h_scr[HALO:HALO + tm, :] = modulated(x_ref[0])
        h_scr[HALO + tm:, :] = modulated(xn_ref[0])
        h_main = h_scr[HALO:HALO + tm, :]
    else:
        h_scr[...] = modulated(x_ref[0])
        h_main = h_scr[...]

    def w_cols(off, g):
        return w_ref[:, off + g * V7X_MXU_COLS: off + (g + 1) * V7X_MXU_COLS]

    def group(off, g):
        return _dot(h_main, w_cols(off, g))

    k_w = w_cols(OFF_K, 0) if full else w_ref[:, 0:KV_WIDTH]
    v_w = w_cols(OFF_V, 0) if full else w_ref[:, KV_WIDTH:2 * KV_WIDTH]
    def finish_keys(k_raw):
        k = _head_norm_rope(k_raw, kg_ref[...], ones_ref, rope_refs, row0, tm, 1.0)
        for m in range(KV_WIDTH // V7X_LANES):
            k_pair = k[:, m * V7X_LANES:(m + 1) * V7X_LANES]
            k_out[0, 2 * m] = k_pair[:, :HEAD_DIM].astype(BF16)
            k_out[0, 2 * m + 1] = pltpu.roll(k_pair, HEAD_DIM, 1)[:, :HEAD_DIM].astype(BF16)

    def finish_queries(g, q_raw):
        q = _head_norm_rope(q_raw, qg_ref[...], ones_ref, rope_refs, row0, tm, SCORE_SCALE)
        for m in range(V7X_MXU_COLS // V7X_LANES):
            r0 = g * V7X_MXU_COLS + m * V7X_LANES
            qt_out[0, r0:r0 + V7X_LANES, :] = q[:, m * V7X_LANES:(m + 1) * V7X_LANES].T.astype(BF16)

    k_raw = _dot(h_main, k_w)
    v = _dot(h_main, v_w)
    for m in range(KV_WIDTH // V7X_LANES):
        vt_pair = v[:, m * V7X_LANES:(m + 1) * V7X_LANES].T.astype(BF16)
        vt_out[0, 2 * m, 0:HEAD_DIM, :] = vt_pair[:HEAD_DIM]
        vt_out[0, 2 * m + 1, 0:HEAD_DIM, :] = vt_pair[HEAD_DIM:]
    vt_out[0, :, HEAD_DIM:, :] = jnp.ones((N_KV_HEADS, PV_ROWS - HEAD_DIM, tm), BF16)
    if not full:
        finish_keys(k_raw)
        return

    finish_previous = functools.partial(finish_keys, k_raw)
    for g in range(ATTN_WIDTH // V7X_MXU_COLS):
        cols = slice(g * V7X_MXU_COLS, (g + 1) * V7X_MXU_COLS)
        q_raw = group(OFF_Q, g)
        finish_previous()
        finish_previous = functools.partial(finish_queries, g, q_raw)
        sza_out[0, :, cols] = _silu(group(OFF_ZA, g)).astype(BF16)
    finish_previous()

    h_ext = h_scr[...]
    row = lax.broadcasted_iota(jnp.int32, (tm, V7X_MXU_COLS), 0)
    first_row = jnp.logical_and(row == 0, i == 0)
    last_row = jnp.logical_and(row == tm - 1, i == n_tiles - 1)
    for g in range(CONV_WIDTH // V7X_MXU_COLS):
        cols = slice(g * V7X_MXU_COLS, (g + 1) * V7X_MXU_COLS)
        u_scr[...] = _dot(h_ext, w_cols(OFF_CC, g)) * _dot(h_ext, w_cols(OFF_XC, g))
        u_prev = jnp.where(first_row, 0.0, u_scr[HALO - 1:HALO - 1 + tm, :])
        u_next = jnp.where(last_row, 0.0, u_scr[HALO + 1:HALO + 1 + tm, :])
        conv = (u_prev * cw_ref[0:1, cols] + u_scr[HALO:HALO + tm, :] * cw_ref[1:2, cols]
                + u_next * cw_ref[2:3, cols] + cb_ref[:, cols])
        y = group(OFF_BC, g) * conv
        yz_scr[:, cols] = (y * _silu(group(OFF_ZC, g))).astype(BF16)

    yz = yz_scr[...]
    for g in range(D_MODEL // V7X_MXU_COLS):
        cols = slice(g * V7X_MXU_COLS, (g + 1) * V7X_MXU_COLS)
        g0_out[0, :, cols] = jax.nn.sigmoid(group(OFF_GL, g) + bg_ref[0:1, cols]).astype(BF16)
        gate1 = jax.nn.sigmoid(group(OFF_GL + D_MODEL, g) + bg_ref[1:2, cols])
        t1_out[0, :, cols] = gate1 * _dot(yz, wcb_ref[:, cols])


def _const_spec(shape):
    return pl.BlockSpec(shape, lambda b, i: (0,) * len(shape), pipeline_mode=pl.Buffered(1))


def _layer_spec(shape, layer, col_block=0):
    zeros = (0,) * (len(shape) - 1)
    return pl.BlockSpec((None, *shape), lambda b, i: (layer, *zeros, col_block), pipeline_mode=pl.Buffered(1))


def _projection(x, mod, mod_row, layer, p, rope_tabs, *, rope, full):
    bsz, seq, _ = x.shape
    tm = min(PROJ_ROW_TILE, seq)
    n_tiles = seq // tm
    halo_per_tile = tm // HALO
    n_halo = seq // HALO
    mod_base = layer * MOD_ROWS

    def tile_specs(tile_of):
        def mod_spec(part):
            if mod_row is None:
                return pl.BlockSpec((1, 1, D_MODEL), lambda b, i: (mod_base + tile_of(b, i)[0], 0, part))
            return pl.BlockSpec((1, 1, D_MODEL), lambda b, i: (mod_base + mod_row, 0, part))

        def prev_halo(b, i):
            tb, ti = tile_of(b, i)
            return tb, jnp.maximum(ti * halo_per_tile - 1, 0), 0

        def next_halo(b, i):
            tb, ti = tile_of(b, i)
            return tb, jnp.minimum((ti + 1) * halo_per_tile, n_halo - 1), 0

        return [pl.BlockSpec((1, tm, D_MODEL), lambda b, i: (*tile_of(b, i), 0)),
                pl.BlockSpec((1, HALO, D_MODEL), prev_halo), pl.BlockSpec((1, HALO, D_MODEL), next_halo),
                mod_spec(0), mod_spec(1)]


    tab_spec = _const_spec(rope_tabs[0].shape)
    k_spec = pl.BlockSpec((1, N_KV_HEADS, tm, HEAD_DIM), lambda b, i: (b, 0, i, 0))
    vt_spec = pl.BlockSpec((1, N_KV_HEADS, PV_ROWS, tm), lambda b, i: (b, 0, 0, i))
    k_shape = jax.ShapeDtypeStruct((bsz, N_KV_HEADS, seq, HEAD_DIM), BF16)
    vt_shape = jax.ShapeDtypeStruct((bsz, N_KV_HEADS, PV_ROWS, seq), BF16)
    qt_spec = pl.BlockSpec((1, ATTN_WIDTH, tm), lambda b, i: (b, 0, i))
    row_spec = pl.BlockSpec((1, tm, D_MODEL), lambda b, i: (b, i, 0))
    gain_spec = _layer_spec((1, V7X_MXU_COLS), layer)

    kern = functools.partial(_projection_kernel, rope=rope, full=full, tm=tm, n_tiles=n_tiles)
    if full:
        in_specs = [*tile_specs(lambda b, i: (b, i)), _layer_spec((1, D_MODEL), layer),
                    _layer_spec((D_MODEL, PROJ_WIDTH), layer), gain_spec, gain_spec,
                    _const_spec(p["ones"].shape), tab_spec, tab_spec, tab_spec,
                    _layer_spec(p["conv_w"].shape[1:], layer), _layer_spec((1, CONV_WIDTH), layer),
                    _layer_spec((CONV_WIDTH, D_MODEL), layer), _layer_spec(p["b_gate"].shape[1:], layer)]
        args = (x, x, x, mod, mod, p["norm_g"], p["w_in"], p["q_gain"], p["k_gain"], p["ones"], *rope_tabs,
                p["conv_w"], p["conv_b"], p["w_conv_br"], p["b_gate"])
        out_specs = [qt_spec, k_spec, vt_spec, row_spec, row_spec, row_spec]
        out_shape = [jax.ShapeDtypeStruct((bsz, ATTN_WIDTH, seq), BF16), k_shape, vt_shape,
                     jax.ShapeDtypeStruct((bsz, seq, ATTN_WIDTH), BF16),
                     jax.ShapeDtypeStruct((bsz, seq, D_MODEL), BF16),
                     jax.ShapeDtypeStruct((bsz, seq, D_MODEL), F32)]
        scratch = [pltpu.VMEM((tm + 2 * HALO, D_MODEL), BF16),
                   pltpu.VMEM((tm + 2 * HALO, V7X_MXU_COLS), F32),
                   pltpu.VMEM((tm, CONV_WIDTH), BF16)]
    else:
        kv_cols = 2 * KV_WIDTH
        x_spec, _, _, shift_spec, scale_spec = tile_specs(lambda b, i: (b, i))
        in_specs = [x_spec, shift_spec, scale_spec, _layer_spec((1, D_MODEL), layer),
                    _layer_spec((D_MODEL, kv_cols), layer, col_block=OFF_K // kv_cols),
                    gain_spec, _const_spec(p["ones"].shape), tab_spec, tab_spec, tab_spec]
        args = (x, mod, mod, p["norm_g"], p["w_in"], p["k_gain"], p["ones"], *rope_tabs)
        out_specs = [k_spec, vt_spec]
        out_shape = [k_shape, vt_shape]
        scratch = [pltpu.VMEM((tm, D_MODEL), BF16)]
    return pl.pallas_call(
        kern,
        grid=(bsz, n_tiles),
        in_specs=in_specs,
        out_specs=out_specs,
        out_shape=out_shape,
        scratch_shapes=scratch,
        compiler_params=pltpu.CompilerParams(dimension_semantics=("arbitrary", "arbitrary"),
                                             vmem_limit_bytes=PROJ_VMEM_LIMIT,
                                             allow_input_fusion=[a is p["w_in"] for a in args]),
        name="projection_full" if full else "projection_kv",
    )(*args)


def _attention_kernel(*refs, n_seg):
    slot0_ref, qt_ref = refs[0], refs[1]
    refs = refs[1:]
    k_refs, vt_refs = refs[1:1 + n_seg], refs[1 + n_seg:1 + 2 * n_seg]
    (sza_ref, g0_ref, t1_ref, x_ref, gate_ref, wab_ref, wout_ref, o_ref, attn_scr, st_scr) = refs[1 + 2 * n_seg:]
    slot0 = slot0_ref[0]
    tiles, ring0 = [], 0
    for seg in range(n_seg):
        seg_keys = k_refs[seg].shape[2]
        size = min(KEY_TILE, seg_keys)
        for start in range(0, seg_keys, size):
            tiles.append((seg, slice(start, start + size), slice(ring0 + start, ring0 + start + size)))
        ring0 += seg_keys
    n_tiles = len(tiles)
    group = N_Q_HEADS // N_KV_HEADS
    sublanes = 8
    slots = st_scr.shape[0]
    tq = st_scr.shape[2]
    units = [(sub, h) for sub in range(qt_ref.shape[2] // tq) for h in range(N_Q_HEADS)]

    def score_tile(g, t, m8):
        sub, h = units[g]
        seg, seg_rows, ring_rows = tiles[t]
        s = _dot(k_refs[seg][0, h // group, seg_rows, :],
                 qt_ref[0, h * HEAD_DIM:(h + 1) * HEAD_DIM, sub * tq:(sub + 1) * tq])
        st_scr[slot0 + g % slots, ring_rows, :] = s.astype(BF16)
        for r in range(s.shape[0] // sublanes):
            piece = s[r * sublanes:(r + 1) * sublanes]
            m8 = piece if m8 is None else jnp.maximum(m8, piece)
        return m8

    def pv_tile(g, t, m_row, acc):
        seg, seg_rows, ring_rows = tiles[t]
        kv = units[g][1] // group
        p = jnp.exp2(st_scr[slot0 + g % slots, ring_rows, :] - m_row)
        part = _dot(vt_refs[seg][0, kv, :, seg_rows], p)
        return part if acc is None else acc + part

    def merge_stages(sub):
        rows = slice(sub * tq, (sub + 1) * tq)
        state = {}

        def branch():
            state["br"] = _dot(attn_scr[sub], wab_ref[...])

        def project():
            merged = g0_ref[0, rows, :].astype(F32) * state.pop("br") + t1_ref[0, rows, :]
            state["out"] = _dot(merged.astype(BF16), wout_ref[...])

        def residual():
            o_ref[0, rows, :] = x_ref[0, rows, :] + gate_ref[0] * state.pop("out")

        return [branch, project, residual]

    col_max = {}
    for g in range(min(SCORE_LOOKAHEAD, len(units))):
        m8 = None
        for t in range(n_tiles):
            m8 = score_tile(g, t, m8)
        col_max[g] = jnp.max(m8, axis=0, keepdims=True).astype(BF16)
    halves, pending = [], []
    for g, (sub, h) in enumerate(units):
        if pending:
            pending.pop(0)()
        acc, m8 = None, None
        ahead = g + SCORE_LOOKAHEAD
        m_row = col_max.pop(g)
        for t in range(n_tiles):
            if ahead < len(units):
                m8 = score_tile(ahead, t, m8)
            acc = pv_tile(g, t, m_row, acc)
        if ahead < len(units):
            col_max[ahead] = jnp.max(m8, axis=0, keepdims=True).astype(BF16)
        halves.append(acc[:HEAD_DIM] / acc[HEAD_DIM:HEAD_DIM + 1])
        if len(halves) == V7X_LANES // HEAD_DIM:
            j = h // len(halves)
            lanes = slice(j * V7X_LANES, (j + 1) * V7X_LANES)
            attn = jnp.concatenate(halves, axis=0).T
            gate = sza_ref[0, sub * tq:(sub + 1) * tq, lanes].astype(F32)
            attn_scr[sub, :, lanes] = (attn * gate).astype(BF16)
            halves = []
        if h == N_Q_HEADS - 1:
            pending += merge_stages(sub)
    for stage in pending:
        stage()


def _attention(qt, ks, vts, sza, g0, t1, x, mod, mod_row, layer, w_attn_br, w_out):
    bsz, seq, _ = x.shape
    sub_rows = min(QUERY_SUBTILE, seq)
    tq = min(ATTN_SUBTILES * sub_rows, seq)
    n_keys = sum(k.shape[2] for k in ks)
    mod_base = layer * MOD_ROWS
    k_specs = [pl.BlockSpec((1, N_KV_HEADS, k.shape[2], HEAD_DIM), lambda b, i: (b, 0, 0, 0)) for k in ks]
    vt_specs = [pl.BlockSpec((1, N_KV_HEADS, PV_ROWS, vt.shape[3]), lambda b, i: (b, 0, 0, 0)) for vt in vts]
    row_spec = pl.BlockSpec((1, tq, D_MODEL), lambda b, i: (b, i, 0))
    if mod_row is None:
        gate_spec = pl.BlockSpec((1, 1, D_MODEL), lambda b, i: (mod_base + b, 0, 2))
    else:
        gate_spec = pl.BlockSpec((1, 1, D_MODEL), lambda b, i: (mod_base + mod_row, 0, 2))
    return pl.pallas_call(
        functools.partial(_attention_kernel, n_seg=len(ks)),
        grid=(bsz, seq // tq),
        in_specs=[pl.BlockSpec(memory_space=pltpu.SMEM),
                  pl.BlockSpec((1, ATTN_WIDTH, tq), lambda b, i: (b, 0, i)), *k_specs, *vt_specs,
                  row_spec, row_spec, row_spec, row_spec, gate_spec,
                  _layer_spec(w_attn_br.shape[1:], layer), _layer_spec(w_out.shape[1:], layer)],
        out_specs=row_spec,
        out_shape=jax.ShapeDtypeStruct(x.shape, F32),
        scratch_shapes=[pltpu.VMEM((tq // sub_rows, sub_rows, ATTN_WIDTH), BF16),
                        pltpu.VMEM((SCORE_LOOKAHEAD + 1, n_keys, sub_rows), BF16)],
        compiler_params=pltpu.CompilerParams(dimension_semantics=("arbitrary", "arbitrary"),
                                             vmem_limit_bytes=ATTN_VMEM_LIMIT),
        name="attention_merge",
    )(jnp.zeros((1,), jnp.int32), qt, *ks, *vts, sza, g0, t1, x, mod, w_attn_br, w_out)


def _rope_tables(n_tokens):
    t = np.arange(n_tokens)
    row = (t // GRID_W).astype(np.float32)
    col = (t % GRID_W).astype(np.float32)
    half = HEAD_DIM // 2
    inv = (np.float32(ROPE_THETA) ** (-np.arange(0, half, 2, dtype=np.float32) / np.float32(half))).astype(np.float32)
    ang_r = row[:, None] * inv[None, :]
    ang_c = col[:, None] * inv[None, :]
    ang = np.concatenate([ang_r, ang_r, ang_c, ang_c], axis=-1).astype(np.float64)
    cos, sin = np.cos(ang), np.sin(ang)
    first = (np.arange(HEAD_DIM) % half) < half // 2
    sin_lo = np.where(first[None, :], -sin, 0.0)
    sin_hi = np.where(first[None, :], 0.0, sin)
    reps = V7X_LANES // HEAD_DIM
    return tuple(jnp.asarray(np.tile(a, (1, reps)), dtype=F32) for a in (cos, sin_lo, sin_hi))


def _identity_rope_tables(n_tokens):
    one = jnp.ones((n_tokens, V7X_LANES), F32)
    zero = jnp.zeros((n_tokens, V7X_LANES), F32)
    return one, zero, zero


def _head_sum_matrix():
    idx = np.arange(V7X_MXU_COLS) // HEAD_DIM
    return jnp.asarray(idx[:, None] == idx[None, :], dtype=BF16)


def kernel(x, c, ctx, c_ctx, norm_g, w_ada, b_ada, w_in, q_norm_g, k_norm_g, conv_w, conv_b, w_attn_br, w_conv_br,
           b_gate, w_out):
    depth = w_ada.shape[0]
    bsz, seq, _ = x.shape
    ctx_len = ctx.shape[1]
    assert bsz + 1 <= MOD_ROWS and w_ada.shape[2] % MOD_COLS == 0
    assert seq % PROJ_ROW_TILE == 0 and seq % (ATTN_SUBTILES * QUERY_SUBTILE) == 0 and seq % KEY_TILE == 0
    assert ctx_len <= min(PROJ_ROW_TILE, QUERY_SUBTILE, KEY_TILE) and ctx_len % V7X_LANES == 0

    cond = jnp.concatenate([c, c_ctx[None, :], jnp.zeros((MOD_ROWS - bsz - 1, D_MODEL), F32)], axis=0)
    mod = _modulation(cond, w_ada, b_ada).reshape(depth * MOD_ROWS, 1, 3 * D_MODEL)
    ctx_row = bsz

    rope_x = _rope_tables(seq)
    rope_c = _identity_rope_tables(ctx_len)
    reps = V7X_MXU_COLS // HEAD_DIM
    p = dict(
        norm_g=norm_g.reshape(depth, 1, D_MODEL),
        w_in=w_in.astype(BF16),
        q_gain=jnp.tile(q_norm_g, (1, reps)).reshape(depth, 1, V7X_MXU_COLS),
        k_gain=jnp.tile(k_norm_g, (1, reps)).reshape(depth, 1, V7X_MXU_COLS),
        ones=_head_sum_matrix(),
        conv_w=conv_w,
        conv_b=conv_b.reshape(depth, 1, CONV_WIDTH),
        w_conv_br=w_conv_br.astype(BF16),
        b_gate=b_gate,
    )
    wab = w_attn_br.astype(BF16)
    wo = w_out.astype(BF16)

    for l in range(depth):
        last = l == depth - 1
        qx, ktx, vx, szax, g0x, t1x = _projection(x, mod, None, l, p, rope_x, rope=True, full=True)
        if last:
            ktc, vc = _projection(ctx, mod, ctx_row, l, p, rope_c, rope=False, full=False)
        else:
            qc, ktc, vc, szac, g0c, t1c = _projection(ctx, mod, ctx_row, l, p, rope_c, rope=False, full=True)
        x_new = _attention(qx, [ktc, ktx], [vc, vx], szax, g0x, t1x, x, mod, None, l, wab, wo)
        if not last:
            ctx = _attention(qc, [ktc], [vc], szac, g0c, t1c, ctx, mod, ctx_row, l, wab, wo)
        x = x_new
    return x
```

```python
import functools

import numpy as np
import jax
import jax.numpy as jnp
from jax import lax
from jax.experimental import pallas as pl
from jax.experimental.pallas import tpu as pltpu

D_MODEL = 1024
GRID_W = 64
HEAD_DIM = 64
N_Q_HEADS = 16
N_KV_HEADS = 4
ATTN_WIDTH = N_Q_HEADS * HEAD_DIM
KV_WIDTH = N_KV_HEADS * HEAD_DIM
CONV_WIDTH = D_MODEL
ROPE_THETA = 10000.0
EPS = 1e-6
ATTN_SCALE = HEAD_DIM ** -0.5
SCORE_SCALE = ATTN_SCALE * 1.4426950408889634

OFF_Q = 0
OFF_K = OFF_Q + ATTN_WIDTH
OFF_V = OFF_K + KV_WIDTH
OFF_ZA = OFF_V + KV_WIDTH
OFF_XC = OFF_ZA + ATTN_WIDTH
OFF_BC = OFF_XC + CONV_WIDTH
OFF_CC = OFF_BC + CONV_WIDTH
OFF_ZC = OFF_CC + CONV_WIDTH
OFF_GL = OFF_ZC + CONV_WIDTH
PROJ_WIDTH = OFF_GL + 2 * D_MODEL

V7X_LANES = 128
V7X_MXU_COLS = 256
V7X_BF16_SUBLANES = 16
V7X_VMEM_BYTES = 64 * 1024 * 1024
HALO = V7X_BF16_SUBLANES
PROJ_ROW_TILE = 512
QUERY_SUBTILE = V7X_MXU_COLS
ATTN_SUBTILES = 2
PV_ROWS = HEAD_DIM + V7X_BF16_SUBLANES
KEY_TILE = V7X_MXU_COLS
SCORE_LOOKAHEAD = 2
W_STAGE_COLS = 512
W_STAGE_SLOTS = 3
MOD_ROWS = 16
MOD_COLS = 1024
PROJ_VMEM_LIMIT = V7X_VMEM_BYTES * 7 // 8
ATTN_VMEM_LIMIT = V7X_VMEM_BYTES * 3 // 4

BF16 = jnp.bfloat16
F32 = jnp.float32


def _dot(a, b):
    return jnp.dot(a, b, preferred_element_type=F32)


def _silu(x):
    return x * jax.nn.sigmoid(x)


def _modulation_kernel(cond_ref, w_ref, b_ref, o_ref):
    a = _silu(cond_ref[...]).astype(BF16)
    o_ref[0] = _dot(a, w_ref[0].astype(BF16)) + b_ref[0]


def _modulation(cond, w_ada, b_ada):
    depth = w_ada.shape[0]
    n_out = w_ada.shape[2]
    return pl.pallas_call(
        _modulation_kernel,
        grid=(depth, n_out // MOD_COLS),
        in_specs=[
            pl.BlockSpec((MOD_ROWS, D_MODEL), lambda l, j: (0, 0)),
            pl.BlockSpec((1, D_MODEL, MOD_COLS), lambda l, j: (l, 0, j)),
            pl.BlockSpec((1, 1, MOD_COLS), lambda l, j: (l, 0, j)),
        ],
        out_specs=pl.BlockSpec((1, MOD_ROWS, MOD_COLS), lambda l, j: (l, 0, j)),
        out_shape=jax.ShapeDtypeStruct((depth, MOD_ROWS, n_out), F32),
        name="modulation",
    )(cond, w_ada, b_ada.reshape(depth, 1, n_out))


def _head_norm_rope(raw, gain, ones_ref, rope_refs, row0, tm, out_scale):
    sumsq = _dot((raw * raw).astype(BF16), ones_ref[...])
    r = lax.rsqrt(sumsq * (1.0 / HEAD_DIM) + EPS)
    a = raw * gain
    if rope_refs is None:
        return a * (r * out_scale) if out_scale != 1.0 else a * r
    cos_ref, sin_lo_ref, sin_hi_ref = rope_refs
    rows = pl.ds(row0, tm)
    cos, sin_lo, sin_hi = cos_ref[rows, :], sin_lo_ref[rows, :], sin_hi_ref[rows, :]
    parts = []
    for m in range(V7X_MXU_COLS // V7X_LANES):
        c = a[:, m * V7X_LANES:(m + 1) * V7X_LANES]
        up = pltpu.roll(c, V7X_LANES - HEAD_DIM // 4, 1)
        down = pltpu.roll(c, HEAD_DIM // 4, 1)
        parts.append(c * cos + up * sin_lo + down * sin_hi)
    rot = jnp.concatenate(parts, axis=1)
    return rot * (r * out_scale) if out_scale != 1.0 else rot * r


def _projection_kernel(*refs, rope, full, tm, n_tiles, layer):
    if full:
        (x_ref, xp_ref, xn_ref, shift_ref, scale_ref, ng_ref, w_hbm, qg_ref, kg_ref, ones_ref,
         cos_ref, sin_lo_ref, sin_hi_ref, cw_ref, cb_ref, wcb_ref, bg_ref,
         qt_out, k_out, vt_out, sza_out, g0_out, t1_out, h_scr, u_scr, yz_scr,
         w_ref, stage_scr, stage_sem) = refs
    else:
        (x_ref, shift_ref, scale_ref, ng_ref, w_hbm, kg_ref, ones_ref,
         cos_ref, sin_lo_ref, sin_hi_ref, k_out, vt_out, h_scr, w_ref, stage_scr, stage_sem) = refs
    i = pl.program_id(1)

    first_col = 0 if full else OFF_K
    stage_slots, _, stage_cols = stage_scr.shape
    n_chunks = w_ref.shape[1] // stage_cols

    def chunk_copy(c):
        slot = c % stage_slots
        return pltpu.make_async_copy(w_hbm.at[layer, :, pl.ds(first_col + c * stage_cols, stage_cols)],
                                     stage_scr.at[slot], stage_sem.at[slot])

    @pl.when(jnp.logical_and(pl.program_id(0) == 0, i == 0))
    def _():
        in_flight = stage_slots - 1
        for c in range(min(in_flight, n_chunks)):
            chunk_copy(c).start()
        for c in range(n_chunks):
            if c + in_flight < n_chunks:
                chunk_copy(c + in_flight).start()
            chunk_copy(c).wait()
            w_ref[:, c * stage_cols:(c + 1) * stage_cols] = stage_scr[c % stage_slots].astype(BF16)
    rope_refs = (cos_ref, sin_lo_ref, sin_hi_ref) if rope else None
    row0 = pl.multiple_of(i * tm, tm)

    mod_gain = ng_ref[...] * (1.0 + scale_ref[0])
    mod_shift = shift_ref[0]

    def modulated(xv):
        ms = jnp.mean(xv * xv, axis=-1, keepdims=True)
        return ((xv * lax.rsqrt(ms + EPS)) * mod_gain + mod_shift).astype(BF16)

    if full:
        h_scr[0:HALO, :] = modulated(xp_ref[0])
        h_scr[HALO:HALO + tm, :] = modulated(x_ref[0])
        h_scr[HALO + tm:, :] = modulated(xn_ref[0])
        h_main = h_scr[HALO:HALO + tm, :]
    else:
        h_scr[...] = modulated(x_ref[0])
        h_main = h_scr[...]

    def w_cols(off, g):
        return w_ref[:, off + g * V7X_MXU_COLS: off + (g + 1) * V7X_MXU_COLS]

    def group(off, g):
        return _dot(h_main, w_cols(off, g))

    k_w = w_cols(OFF_K, 0) if full else w_ref[:, 0:KV_WIDTH]
    v_w = w_cols(OFF_V, 0) if full else w_ref[:, KV_WIDTH:2 * KV_WIDTH]
    def finish_keys(k_raw):
        k = _head_norm_rope(k_raw, kg_ref[...], ones_ref, rope_refs, row0, tm, 1.0)
        for m in range(KV_WIDTH // V7X_LANES):
            k_pair = k[:, m * V7X_LANES:(m + 1) * V7X_LANES]
            k_out[0, 2 * m] = k_pair[:, :HEAD_DIM].astype(BF16)
            k_out[0, 2 * m + 1] = pltpu.roll(k_pair, HEAD_DIM, 1)[:, :HEAD_DIM].astype(BF16)

    def finish_queries(g, q_raw):
        q = _head_norm_rope(q_raw, qg_ref[...], ones_ref, rope_refs, row0, tm, SCORE_SCALE)
        for m in range(V7X_MXU_COLS // V7X_LANES):
            r0 = g * V7X_MXU_COLS + m * V7X_LANES
            qt_out[0, r0:r0 + V7X_LANES, :] = q[:, m * V7X_LANES:(m + 1) * V7X_LANES].T.astype(BF16)

    k_raw = _dot(h_main, k_w)
    v = _dot(h_main, v_w)
    for m in range(KV_WIDTH // V7X_LANES):
        vt_pair = v[:, m * V7X_LANES:(m + 1) * V7X_LANES].T.astype(BF16)
        vt_out[0, 2 * m, 0:HEAD_DIM, :] = vt_pair[:HEAD_DIM]
        vt_out[0, 2 * m + 1, 0:HEAD_DIM, :] = vt_pair[HEAD_DIM:]
    vt_out[0, :, HEAD_DIM:, :] = jnp.ones((N_KV_HEADS, PV_ROWS - HEAD_DIM, tm), BF16)
    if not full:
        finish_keys(k_raw)
        return

    finish_previous = functools.partial(finish_keys, k_raw)
    for g in range(ATTN_WIDTH // V7X_MXU_COLS):
        cols = slice(g * V7X_MXU_COLS, (g + 1) * V7X_MXU_COLS)
        q_raw = group(OFF_Q, g)
        finish_previous()
        finish_previous = functools.partial(finish_queries, g, q_raw)
        sza_out[0, :, cols] = _silu(group(OFF_ZA, g)).astype(BF16)
    finish_previous()

    h_ext = h_scr[...]
    row = lax.broadcasted_iota(jnp.int32, (tm, V7X_MXU_COLS), 0)
    first_row = jnp.logical_and(row == 0, i == 0)
    last_row = jnp.logical_and(row == tm - 1, i == n_tiles - 1)
    for g in range(CONV_WIDTH // V7X_MXU_COLS):
        cols = slice(g * V7X_MXU_COLS, (g + 1) * V7X_MXU_COLS)
        u_scr[...] = _dot(h_ext, w_cols(OFF_CC, g)) * _dot(h_ext, w_cols(OFF_XC, g))
        u_prev = jnp.where(first_row, 0.0, u_scr[HALO - 1:HALO - 1 + tm, :])
        u_next = jnp.where(last_row, 0.0, u_scr[HALO + 1:HALO + 1 + tm, :])
        conv = (u_prev * cw_ref[0:1, cols] + u_scr[HALO:HALO + tm, :] * cw_ref[1:2, cols]
                + u_next * cw_ref[2:3, cols] + cb_ref[:, cols])
        y = group(OFF_BC, g) * conv
        yz_scr[:, cols] = (y * _silu(group(OFF_ZC, g))).astype(BF16)

    yz = yz_scr[...]
    for g in range(D_MODEL // V7X_MXU_COLS):
        cols = slice(g * V7X_MXU_COLS, (g + 1) * V7X_MXU_COLS)
        g0_out[0, :, cols] = jax.nn.sigmoid(group(OFF_GL, g) + bg_ref[0:1, cols]).astype(BF16)
        gate1 = jax.nn.sigmoid(group(OFF_GL + D_MODEL, g) + bg_ref[1:2, cols])
        t1_out[0, :, cols] = gate1 * _dot(yz, wcb_ref[:, cols])


def _const_spec(shape):
    return pl.BlockSpec(shape, lambda b, i: (0,) * len(shape), pipeline_mode=pl.Buffered(1))


def _layer_spec(shape, layer, col_block=0):
    zeros = (0,) * (len(shape) - 1)
    return pl.BlockSpec((None, *shape), lambda b, i: (layer, *zeros, col_block), pipeline_mode=pl.Buffered(1))


def _projection(x, mod, mod_row, layer, p, rope_tabs, *, rope, full):
    bsz, seq, _ = x.shape
    tm = min(PROJ_ROW_TILE, seq)
    n_tiles = seq // tm
    halo_per_tile = tm // HALO
    n_halo = seq // HALO
    mod_base = layer * MOD_ROWS

    def tile_specs(tile_of):
        def mod_spec(part):
            if mod_row is None:
                return pl.BlockSpec((1, 1, D_MODEL), lambda b, i: (mod_base + tile_of(b, i)[0], 0, part))
            return pl.BlockSpec((1, 1, D_MODEL), lambda b, i: (mod_base + mod_row, 0, part))

        def prev_halo(b, i):
            tb, ti = tile_of(b, i)
            return tb, jnp.maximum(ti * halo_per_tile - 1, 0), 0

        def next_halo(b, i):
            tb, ti = tile_of(b, i)
            return tb, jnp.minimum((ti + 1) * halo_per_tile, n_halo - 1), 0

        return [pl.BlockSpec((1, tm, D_MODEL), lambda b, i: (*tile_of(b, i), 0)),
                pl.BlockSpec((1, HALO, D_MODEL), prev_halo), pl.BlockSpec((1, HALO, D_MODEL), next_halo),
                mod_spec(0), mod_spec(1)]


    tab_spec = _const_spec(rope_tabs[0].shape)
    k_spec = pl.BlockSpec((1, N_KV_HEADS, tm, HEAD_DIM), lambda b, i: (b, 0, i, 0))
    vt_spec = pl.BlockSpec((1, N_KV_HEADS, PV_ROWS, tm), lambda b, i: (b, 0, 0, i))
    k_shape = jax.ShapeDtypeStruct((bsz, N_KV_HEADS, seq, HEAD_DIM), BF16)
    vt_shape = jax.ShapeDtypeStruct((bsz, N_KV_HEADS, PV_ROWS, seq), BF16)
    qt_spec = pl.BlockSpec((1, ATTN_WIDTH, tm), lambda b, i: (b, 0, i))
    row_spec = pl.BlockSpec((1, tm, D_MODEL), lambda b, i: (b, i, 0))
    gain_spec = _layer_spec((1, V7X_MXU_COLS), layer)

    kern = functools.partial(_projection_kernel, rope=rope, full=full, tm=tm, n_tiles=n_tiles, layer=layer)
    w_spec = pl.BlockSpec(memory_space=pl.ANY)

    def weight_scratch(n_cols):
        return [pltpu.VMEM((D_MODEL, n_cols), BF16), pltpu.VMEM((W_STAGE_SLOTS, D_MODEL, W_STAGE_COLS), F32),
                pltpu.SemaphoreType.DMA((W_STAGE_SLOTS,))]
    if full:
        in_specs = [*tile_specs(lambda b, i: (b, i)), _layer_spec((1, D_MODEL), layer),
                    w_spec, gain_spec, gain_spec,
                    _const_spec(p["ones"].shape), tab_spec, tab_spec, tab_spec,
                    _layer_spec(p["conv_w"].shape[1:], layer), _layer_spec((1, CONV_WIDTH), layer),
                    _layer_spec((CONV_WIDTH, D_MODEL), layer), _layer_spec(p["b_gate"].shape[1:], layer)]
        args = (x, x, x, mod, mod, p["norm_g"], p["w_in"], p["q_gain"], p["k_gain"], p["ones"], *rope_tabs,
                p["conv_w"], p["conv_b"], p["w_conv_br"], p["b_gate"])
        out_specs = [qt_spec, k_spec, vt_spec, row_spec, row_spec, row_spec]
        out_shape = [jax.ShapeDtypeStruct((bsz, ATTN_WIDTH, seq), BF16), k_shape, vt_shape,
                     jax.ShapeDtypeStruct((bsz, seq, ATTN_WIDTH), BF16),
                     jax.ShapeDtypeStruct((bsz, seq, D_MODEL), BF16),
                     jax.ShapeDtypeStruct((bsz, seq, D_MODEL), F32)]
        scratch = [pltpu.VMEM((tm + 2 * HALO, D_MODEL), BF16),
                   pltpu.VMEM((tm + 2 * HALO, V7X_MXU_COLS), F32),
                   pltpu.VMEM((tm, CONV_WIDTH), BF16), *weight_scratch(PROJ_WIDTH)]
    else:
        kv_cols = 2 * KV_WIDTH
        x_spec, _, _, shift_spec, scale_spec = tile_specs(lambda b, i: (b, i))
        in_specs = [x_spec, shift_spec, scale_spec, _layer_spec((1, D_MODEL), layer),
                    w_spec,
                    gain_spec, _const_spec(p["ones"].shape), tab_spec, tab_spec, tab_spec]
        args = (x, mod, mod, p["norm_g"], p["w_in"], p["k_gain"], p["ones"], *rope_tabs)
        out_specs = [k_spec, vt_spec]
        out_shape = [k_shape, vt_shape]
        scratch = [pltpu.VMEM((tm, D_MODEL), BF16), *weight_scratch(kv_cols)]
    return pl.pallas_call(
        kern,
        grid=(bsz, n_tiles),
        in_specs=in_specs,
        out_specs=out_specs,
        out_shape=out_shape,
        scratch_shapes=scratch,
        compiler_params=pltpu.CompilerParams(dimension_semantics=("arbitrary", "arbitrary"),
                                             vmem_limit_bytes=PROJ_VMEM_LIMIT),
        name="projection_full" if full else "projection_kv",
    )(*args)


def _attention_kernel(*refs, n_seg):
    slot0_ref, qt_ref = refs[0], refs[1]
    refs = refs[1:]
    k_refs, vt_refs = refs[1:1 + n_seg], refs[1 + n_seg:1 + 2 * n_seg]
    (sza_ref, g0_ref, t1_ref, x_ref, gate_ref, wab_ref, wout_ref, o_ref, attn_scr, st_scr) = refs[1 + 2 * n_seg:]
    slot0 = slot0_ref[0]
    tiles, ring0 = [], 0
    for seg in range(n_seg):
        seg_keys = k_refs[seg].shape[2]
        size = min(KEY_TILE, seg_keys)
        for start in range(0, seg_keys, size):
            tiles.append((seg, slice(start, start + size), slice(ring0 + start, ring0 + start + size)))
        ring0 += seg_keys
    n_tiles = len(tiles)
    group = N_Q_HEADS // N_KV_HEADS
    sublanes = 8
    slots = st_scr.shape[0]
    tq = st_scr.shape[2]
    units = [(sub, h) for sub in range(qt_ref.shape[2] // tq) for h in range(N_Q_HEADS)]

    def score_tile(g, t, m8):
        sub, h = units[g]
        seg, seg_rows, ring_rows = tiles[t]
        s = _dot(k_refs[seg][0, h // group, seg_rows, :],
                 qt_ref[0, h * HEAD_DIM:(h + 1) * HEAD_DIM, sub * tq:(sub + 1) * tq])
        st_scr[slot0 + g % slots, ring_rows, :] = s.astype(BF16)
        for r in range(s.shape[0] // sublanes):
            piece = s[r * sublanes:(r + 1) * sublanes]
            m8 = piece if m8 is None else jnp.maximum(m8, piece)
        return m8

    def pv_tile(g, t, m_row, acc):
        seg, seg_rows, ring_rows = tiles[t]
        kv = units[g][1] // group
        p = jnp.exp2(st_scr[slot0 + g % slots, ring_rows, :] - m_row)
        part = _dot(vt_refs[seg][0, kv, :, seg_rows], p)
        return part if acc is None else acc + part

    def merge_stages(sub):
        rows = slice(sub * tq, (sub + 1) * tq)
        state = {}

        def branch():
            state["br"] = _dot(attn_scr[sub], wab_ref[...])

        def project():
            merged = g0_ref[0, rows, :].astype(F32) * state.pop("br") + t1_ref[0, rows, :]
            state["out"] = _dot(merged.astype(BF16), wout_ref[...])

        def residual():
            o_ref[0, rows, :] = x_ref[0, rows, :] + gate_ref[0] * state.pop("out")

        return [branch, project, residual]

    col_max = {}
    for g in range(min(SCORE_LOOKAHEAD, len(units))):
        m8 = None
        for t in range(n_tiles):
            m8 = score_tile(g, t, m8)
        col_max[g] = jnp.max(m8, axis=0, keepdims=True).astype(BF16)
    halves, pending = [], []
    for g, (sub, h) in enumerate(units):
        if pending:
            pending.pop(0)()
        acc, m8 = None, None
        ahead = g + SCORE_LOOKAHEAD
        m_row = col_max.pop(g)
        for t in range(n_tiles):
            if ahead < len(units):
                m8 = score_tile(ahead, t, m8)
            acc = pv_tile(g, t, m_row, acc)
        if ahead < len(units):
            col_max[ahead] = jnp.max(m8, axis=0, keepdims=True).astype(BF16)
        halves.append(acc[:HEAD_DIM] / acc[HEAD_DIM:HEAD_DIM + 1])
        if len(halves) == V7X_LANES // HEAD_DIM:
            j = h // len(halves)
            lanes = slice(j * V7X_LANES, (j + 1) * V7X_LANES)
            attn = jnp.concatenate(halves, axis=0).T
            gate = sza_ref[0, sub * tq:(sub + 1) * tq, lanes].astype(F32)
            attn_scr[sub, :, lanes] = (attn * gate).astype(BF16)
            halves = []
        if h == N_Q_HEADS - 1:
            pending += merge_stages(sub)
    for stage in pending:
        stage()


def _attention(qt, ks, vts, sza, g0, t1, x, mod, mod_row, layer, w_attn_br, w_out):
    bsz, seq, _ = x.shape
    sub_rows = min(QUERY_SUBTILE, seq)
    tq = min(ATTN_SUBTILES * sub_rows, seq)
    n_keys = sum(k.shape[2] for k in ks)
    mod_base = layer * MOD_ROWS
    k_specs = [pl.BlockSpec((1, N_KV_HEADS, k.shape[2], HEAD_DIM), lambda b, i: (b, 0, 0, 0)) for k in ks]
    vt_specs = [pl.BlockSpec((1, N_KV_HEADS, PV_ROWS, vt.shape[3]), lambda b, i: (b, 0, 0, 0)) for vt in vts]
    row_spec = pl.BlockSpec((1, tq, D_MODEL), lambda b, i: (b, i, 0))
    if mod_row is None:
        gate_spec = pl.BlockSpec((1, 1, D_MODEL), lambda b, i: (mod_base + b, 0, 2))
    else:
        gate_spec = pl.BlockSpec((1, 1, D_MODEL), lambda b, i: (mod_base + mod_row, 0, 2))
    return pl.pallas_call(
        functools.partial(_attention_kernel, n_seg=len(ks)),
        grid=(bsz, seq // tq),
        in_specs=[pl.BlockSpec(memory_space=pltpu.SMEM),
                  pl.BlockSpec((1, ATTN_WIDTH, tq), lambda b, i: (b, 0, i)), *k_specs, *vt_specs,
                  row_spec, row_spec, row_spec, row_spec, gate_spec,
                  _layer_spec(w_attn_br.shape[1:], layer), _layer_spec(w_out.shape[1:], layer)],
        out_specs=row_spec,
        out_shape=jax.ShapeDtypeStruct(x.shape, F32),
        scratch_shapes=[pltpu.VMEM((tq // sub_rows, sub_rows, ATTN_WIDTH), BF16),
                        pltpu.VMEM((SCORE_LOOKAHEAD + 1, n_keys, sub_rows), BF16)],
        compiler_params=pltpu.CompilerParams(dimension_semantics=("arbitrary", "arbitrary"),
                                             vmem_limit_bytes=ATTN_VMEM_LIMIT),
        name="attention_merge",
    )(jnp.zeros((1,), jnp.int32), qt, *ks, *vts, sza, g0, t1, x, mod, w_attn_br, w_out)


def _rope_tables(n_tokens):
    t = np.arange(n_tokens)
    row = (t // GRID_W).astype(np.float32)
    col = (t % GRID_W).astype(np.float32)
    half = HEAD_DIM // 2
    inv = (np.float32(ROPE_THETA) ** (-np.arange(0, half, 2, dtype=np.float32) / np.float32(half))).astype(np.float32)
    ang_r = row[:, None] * inv[None, :]
    ang_c = col[:, None] * inv[None, :]
    ang = np.concatenate([ang_r, ang_r, ang_c, ang_c], axis=-1).astype(np.float64)
    cos, sin = np.cos(ang), np.sin(ang)
    first = (np.arange(HEAD_DIM) % half) < half // 2
    sin_lo = np.where(first[None, :], -sin, 0.0)
    sin_hi = np.where(first[None, :], 0.0, sin)
    reps = V7X_LANES // HEAD_DIM
    return tuple(jnp.asarray(np.tile(a, (1, reps)), dtype=F32) for a in (cos, sin_lo, sin_hi))


def _identity_rope_tables(n_tokens):
    one = jnp.ones((n_tokens, V7X_LANES), F32)
    zero = jnp.zeros((n_tokens, V7X_LANES), F32)
    return one, zero, zero


def _head_sum_matrix():
    idx = np.arange(V7X_MXU_COLS) // HEAD_DIM
    return jnp.asarray(idx[:, None] == idx[None, :], dtype=BF16)


def kernel(x, c, ctx, c_ctx, norm_g, w_ada, b_ada, w_in, q_norm_g, k_norm_g, conv_w, conv_b, w_attn_br, w_conv_br,
           b_gate, w_out):
    depth = w_ada.shape[0]
    bsz, seq, _ = x.shape
    ctx_len = ctx.shape[1]
    assert bsz + 1 <= MOD_ROWS and w_ada.shape[2] % MOD_COLS == 0
    assert PROJ_WIDTH % W_STAGE_COLS == 0 and (2 * KV_WIDTH) % W_STAGE_COLS == 0 and OFF_K % V7X_LANES == 0
    assert seq % PROJ_ROW_TILE == 0 and seq % (ATTN_SUBTILES * QUERY_SUBTILE) == 0 and seq % KEY_TILE == 0
    assert ctx_len <= min(PROJ_ROW_TILE, QUERY_SUBTILE, KEY_TILE) and ctx_len % V7X_LANES == 0

    cond = jnp.concatenate([c, c_ctx[None, :], jnp.zeros((MOD_ROWS - bsz - 1, D_MODEL), F32)], axis=0)
    mod = _modulation(cond, w_ada, b_ada).reshape(depth * MOD_ROWS, 1, 3 * D_MODEL)
    ctx_row = bsz

    rope_x = _rope_tables(seq)
    rope_c = _identity_rope_tables(ctx_len)
    reps = V7X_MXU_COLS // HEAD_DIM
    p = dict(
        norm_g=norm_g.reshape(depth, 1, D_MODEL),
        w_in=w_in,
        q_gain=jnp.tile(q_norm_g, (1, reps)).reshape(depth, 1, V7X_MXU_COLS),
        k_gain=jnp.tile(k_norm_g, (1, reps)).reshape(depth, 1, V7X_MXU_COLS),
        ones=_head_sum_matrix(),
        conv_w=conv_w,
        conv_b=conv_b.reshape(depth, 1, CONV_WIDTH),
        w_conv_br=w_conv_br.astype(BF16),
        b_gate=b_gate,
    )
    wab = w_attn_br.astype(BF16)
    wo = w_out.astype(BF16)

    for l in range(depth):
        last = l == depth - 1
        qx, ktx, vx, szax, g0x, t1x = _projection(x, mod, None, l, p, rope_x, rope=True, full=True)
        if last:
            ktc, vc = _projection(ctx, mod, ctx_row, l, p, rope_c, rope=False, full=False)
        else:
            qc, ktc, vc, szac, g0c, t1c = _projection(ctx, mod, ctx_row, l, p, rope_c, rope=False, full=True)
        x_new = _attention(qx, [ktc, ktx], [vc, vx], szax, g0x, t1x, x, mod, None, l, wab, wo)
        if not last:
            ctx = _attention(qc, [ktc], [vc], szac, g0c, t1c, ctx, mod, ctx_row, l, wab, wo)
        x = x_new
    return x
```

```python
import functools

import numpy as np
import jax
import jax.numpy as jnp
from jax import lax
from jax.experimental import pallas as pl
from jax.experimental.pallas import tpu as pltpu

D_MODEL = 1024
GRID_W = 64
HEAD_DIM = 64
N_Q_HEADS = 16
N_KV_HEADS = 4
ATTN_WIDTH = N_Q_HEADS * HEAD_DIM
KV_WIDTH = N_KV_HEADS * HEAD_DIM
CONV_WIDTH = D_MODEL
ROPE_THETA = 10000.0
EPS = 1e-6
ATTN_SCALE = HEAD_DIM ** -0.5
SCORE_SCALE = ATTN_SCALE * 1.4426950408889634

OFF_Q = 0
OFF_K = OFF_Q + ATTN_WIDTH
OFF_V = OFF_K + KV_WIDTH
OFF_ZA = OFF_V + KV_WIDTH
OFF_XC = OFF_ZA + ATTN_WIDTH
OFF_BC = OFF_XC + CONV_WIDTH
OFF_CC = OFF_BC + CONV_WIDTH
OFF_ZC = OFF_CC + CONV_WIDTH
OFF_GL = OFF_ZC + CONV_WIDTH
PROJ_WIDTH = OFF_GL + 2 * D_MODEL

V7X_LANES = 128
V7X_MXU_COLS = 256
V7X_BF16_SUBLANES = 16
V7X_VMEM_BYTES = 64 * 1024 * 1024
HALO = V7X_BF16_SUBLANES
PROJ_ROW_TILE = 512
QUERY_SUBTILE = V7X_MXU_COLS
ATTN_SUBTILES = 2
PV_ROWS = HEAD_DIM + V7X_BF16_SUBLANES
KEY_TILE = V7X_MXU_COLS
SCORE_LOOKAHEAD = 2
W_STAGE_ELEMS = 64 * PROJ_WIDTH
W_STAGE_SLOTS = 3
MOD_ROWS = 16
MOD_COLS = 1024
PROJ_VMEM_LIMIT = V7X_VMEM_BYTES * 7 // 8
ATTN_VMEM_LIMIT = V7X_VMEM_BYTES * 3 // 4

BF16 = jnp.bfloat16
F32 = jnp.float32


def _dot(a, b):
    return jnp.dot(a, b, preferred_element_type=F32)


def _silu(x):
    return x * jax.nn.sigmoid(x)


def _modulation_kernel(cond_ref, w_ref, b_ref, o_ref):
    a = _silu(cond_ref[...]).astype(BF16)
    o_ref[0] = _dot(a, w_ref[0].astype(BF16)) + b_ref[0]


def _modulation(cond, w_ada, b_ada):
    depth = w_ada.shape[0]
    n_out = w_ada.shape[2]
    return pl.pallas_call(
        _modulation_kernel,
        grid=(depth, n_out // MOD_COLS),
        in_specs=[
            pl.BlockSpec((MOD_ROWS, D_MODEL), lambda l, j: (0, 0)),
            pl.BlockSpec((1, D_MODEL, MOD_COLS), lambda l, j: (l, 0, j)),
            pl.BlockSpec((1, 1, MOD_COLS), lambda l, j: (l, 0, j)),
        ],
        out_specs=pl.BlockSpec((1, MOD_ROWS, MOD_COLS), lambda l, j: (l, 0, j)),
        out_shape=jax.ShapeDtypeStruct((depth, MOD_ROWS, n_out), F32),
        name="modulation",
    )(cond, w_ada, b_ada.reshape(depth, 1, n_out))


def _head_norm_rope(raw, gain, ones_ref, rope_refs, row0, tm, out_scale):
    sumsq = _dot((raw * raw).astype(BF16), ones_ref[...])
    r = lax.rsqrt(sumsq * (1.0 / HEAD_DIM) + EPS)
    a = raw * gain
    if rope_refs is None:
        return a * (r * out_scale) if out_scale != 1.0 else a * r
    cos_ref, sin_lo_ref, sin_hi_ref = rope_refs
    rows = pl.ds(row0, tm)
    cos, sin_lo, sin_hi = cos_ref[rows, :], sin_lo_ref[rows, :], sin_hi_ref[rows, :]
    parts = []
    for m in range(V7X_MXU_COLS // V7X_LANES):
        c = a[:, m * V7X_LANES:(m + 1) * V7X_LANES]
        up = pltpu.roll(c, V7X_LANES - HEAD_DIM // 4, 1)
        down = pltpu.roll(c, HEAD_DIM // 4, 1)
        parts.append(c * cos + up * sin_lo + down * sin_hi)
    rot = jnp.concatenate(parts, axis=1)
    return rot * (r * out_scale) if out_scale != 1.0 else rot * r


def _projection_kernel(*refs, rope, full, tm, n_tiles, layer):
    if full:
        (x_ref, xp_ref, xn_ref, shift_ref, scale_ref, ng_ref, w_hbm, qg_ref, kg_ref, ones_ref,
         cos_ref, sin_lo_ref, sin_hi_ref, cw_ref, cb_ref, wcb_ref, bg_ref,
         qt_out, k_out, vt_out, sza_out, g0_out, t1_out, h_scr, u_scr, yz_scr,
         w_ref, stage_scr, stage_sem) = refs
    else:
        (x_ref, shift_ref, scale_ref, ng_ref, w_hbm, kg_ref, ones_ref,
         cos_ref, sin_lo_ref, sin_hi_ref, k_out, vt_out, h_scr, w_ref, stage_scr, stage_sem) = refs
    i = pl.program_id(1)

    first_col = 0 if full else OFF_K
    stage_slots, stage_rows, n_cols = stage_scr.shape
    n_chunks = w_ref.shape[0] // stage_rows

    def chunk_copy(c):
        slot = c % stage_slots
        return pltpu.make_async_copy(w_hbm.at[layer, pl.ds(c * stage_rows, stage_rows), pl.ds(first_col, n_cols)],
                                     stage_scr.at[slot], stage_sem.at[slot])

    @pl.when(jnp.logical_and(pl.program_id(0) == 0, i == 0))
    def _():
        in_flight = stage_slots - 1
        for c in range(min(in_flight, n_chunks)):
            chunk_copy(c).start()
        for c in range(n_chunks):
            if c + in_flight < n_chunks:
                chunk_copy(c + in_flight).start()
            chunk_copy(c).wait()
            w_ref[c * stage_rows:(c + 1) * stage_rows, :] = stage_scr[c % stage_slots].astype(BF16)
    rope_refs = (cos_ref, sin_lo_ref, sin_hi_ref) if rope else None
    row0 = pl.multiple_of(i * tm, tm)

    mod_gain = ng_ref[...] * (1.0 + scale_ref[0])
    mod_shift = shift_ref[0]

    def modulated(xv):
        ms = jnp.mean(xv * xv, axis=-1, keepdims=True)
        return ((xv * lax.rsqrt(ms + EPS)) * mod_gain + mod_shift).astype(BF16)

    if full:
        h_scr[0:HALO, :] = modulated(xp_ref[0])
        h_scr[HALO:HALO + tm, :] = modulated(x_ref[0])
        h_scr[HALO + tm:, :] = modulated(xn_ref[0])
        h_main = h_scr[HALO:HALO + tm, :]
    else:
        h_scr[...] = modulated(x_ref[0])
        h_main = h_scr[...]

    def w_cols(off, g):
        return w_ref[:, off + g * V7X_MXU_COLS: off + (g + 1) * V7X_MXU_COLS]

    def group(off, g):
        return _dot(h_main, w_cols(off, g))

    k_w = w_cols(OFF_K, 0) if full else w_ref[:, 0:KV_WIDTH]
    v_w = w_cols(OFF_V, 0) if full else w_ref[:, KV_WIDTH:2 * KV_WIDTH]
    def finish_keys(k_raw):
        k = _head_norm_rope(k_raw, kg_ref[...], ones_ref, rope_refs, row0, tm, 1.0)
        for m in range(KV_WIDTH // V7X_LANES):
            k_pair = k[:, m * V7X_LANES:(m + 1) * V7X_LANES]
            k_out[0, 2 * m] = k_pair[:, :HEAD_DIM].astype(BF16)
            k_out[0, 2 * m + 1] = pltpu.roll(k_pair, HEAD_DIM, 1)[:, :HEAD_DIM].astype(BF16)

    def finish_queries(g, q_raw):
        q = _head_norm_rope(q_raw, qg_ref[...], ones_ref, rope_refs, row0, tm, SCORE_SCALE)
        for m in range(V7X_MXU_COLS // V7X_LANES):
            r0 = g * V7X_MXU_COLS + m * V7X_LANES
            qt_out[0, r0:r0 + V7X_LANES, :] = q[:, m * V7X_LANES:(m + 1) * V7X_LANES].T.astype(BF16)

    k_raw = _dot(h_main, k_w)
    v = _dot(h_main, v_w)
    for m in range(KV_WIDTH // V7X_LANES):
        vt_pair = v[:, m * V7X_LANES:(m + 1) * V7X_LANES].T.astype(BF16)
        vt_out[0, 2 * m, 0:HEAD_DIM, :] = vt_pair[:HEAD_DIM]
        vt_out[0, 2 * m + 1, 0:HEAD_DIM, :] = vt_pair[HEAD_DIM:]
    vt_out[0, :, HEAD_DIM:, :] = jnp.ones((N_KV_HEADS, PV_ROWS - HEAD_DIM, tm), BF16)
    if not full:
        finish_keys(k_raw)
        return

    finish_previous = functools.partial(finish_keys, k_raw)
    for g in range(ATTN_WIDTH // V7X_MXU_COLS):
        cols = slice(g * V7X_MXU_COLS, (g + 1) * V7X_MXU_COLS)
        q_raw = group(OFF_Q, g)
        finish_previous()
        finish_previous = functools.partial(finish_queries, g, q_raw)
        sza_out[0, :, cols] = _silu(group(OFF_ZA, g)).astype(BF16)
    finish_previous()

    h_ext = h_scr[...]
    row = lax.broadcasted_iota(jnp.int32, (tm, V7X_MXU_COLS), 0)
    first_row = jnp.logical_and(row == 0, i == 0)
    last_row = jnp.logical_and(row == tm - 1, i == n_tiles - 1)
    for g in range(CONV_WIDTH // V7X_MXU_COLS):
        cols = slice(g * V7X_MXU_COLS, (g + 1) * V7X_MXU_COLS)
        u_scr[...] = _dot(h_ext, w_cols(OFF_CC, g)) * _dot(h_ext, w_cols(OFF_XC, g))
        u_prev = jnp.where(first_row, 0.0, u_scr[HALO - 1:HALO - 1 + tm, :])
        u_next = jnp.where(last_row, 0.0, u_scr[HALO + 1:HALO + 1 + tm, :])
        conv = (u_prev * cw_ref[0:1, cols] + u_scr[HALO:HALO + tm, :] * cw_ref[1:2, cols]
                + u_next * cw_ref[2:3, cols] + cb_ref[:, cols])
        y = group(OFF_BC, g) * conv
        yz_scr[:, cols] = (y * _silu(group(OFF_ZC, g))).astype(BF16)

    yz = yz_scr[...]
    for g in range(D_MODEL // V7X_MXU_COLS):
        cols = slice(g * V7X_MXU_COLS, (g + 1) * V7X_MXU_COLS)
        g0_out[0, :, cols] = jax.nn.sigmoid(group(OFF_GL, g) + bg_ref[0:1, cols]).astype(BF16)
        gate1 = jax.nn.sigmoid(group(OFF_GL + D_MODEL, g) + bg_ref[1:2, cols])
        t1_out[0, :, cols] = gate1 * _dot(yz, wcb_ref[:, cols])


def _const_spec(shape):
    return pl.BlockSpec(shape, lambda b, i: (0,) * len(shape), pipeline_mode=pl.Buffered(1))


def _layer_spec(shape, layer, col_block=0):
    zeros = (0,) * (len(shape) - 1)
    return pl.BlockSpec((None, *shape), lambda b, i: (layer, *zeros, col_block), pipeline_mode=pl.Buffered(1))


def _projection(x, mod, mod_row, layer, p, rope_tabs, *, rope, full):
    bsz, seq, _ = x.shape
    tm = min(PROJ_ROW_TILE, seq)
    n_tiles = seq // tm
    halo_per_tile = tm // HALO
    n_halo = seq // HALO
    mod_base = layer * MOD_ROWS

    def tile_specs(tile_of):
        def mod_spec(part):
            if mod_row is None:
                return pl.BlockSpec((1, 1, D_MODEL), lambda b, i: (mod_base + tile_of(b, i)[0], 0, part))
            return pl.BlockSpec((1, 1, D_MODEL), lambda b, i: (mod_base + mod_row, 0, part))

        def prev_halo(b, i):
            tb, ti = tile_of(b, i)
            return tb, jnp.maximum(ti * halo_per_tile - 1, 0), 0

        def next_halo(b, i):
            tb, ti = tile_of(b, i)
            return tb, jnp.minimum((ti + 1) * halo_per_tile, n_halo - 1), 0

        return [pl.BlockSpec((1, tm, D_MODEL), lambda b, i: (*tile_of(b, i), 0)),
                pl.BlockSpec((1, HALO, D_MODEL), prev_halo), pl.BlockSpec((1, HALO, D_MODEL), next_halo),
                mod_spec(0), mod_spec(1)]


    tab_spec = _const_spec(rope_tabs[0].shape)
    k_spec = pl.BlockSpec((1, N_KV_HEADS, tm, HEAD_DIM), lambda b, i: (b, 0, i, 0))
    vt_spec = pl.BlockSpec((1, N_KV_HEADS, PV_ROWS, tm), lambda b, i: (b, 0, 0, i))
    k_shape = jax.ShapeDtypeStruct((bsz, N_KV_HEADS, seq, HEAD_DIM), BF16)
    vt_shape = jax.ShapeDtypeStruct((bsz, N_KV_HEADS, PV_ROWS, seq), BF16)
    qt_spec = pl.BlockSpec((1, ATTN_WIDTH, tm), lambda b, i: (b, 0, i))
    row_spec = pl.BlockSpec((1, tm, D_MODEL), lambda b, i: (b, i, 0))
    gain_spec = _layer_spec((1, V7X_MXU_COLS), layer)

    kern = functools.partial(_projection_kernel, rope=rope, full=full, tm=tm, n_tiles=n_tiles, layer=layer)
    w_spec = pl.BlockSpec(memory_space=pl.ANY)

    def weight_scratch(n_cols):
        rows = min(D_MODEL, W_STAGE_ELEMS // n_cols // V7X_BF16_SUBLANES * V7X_BF16_SUBLANES)
        assert D_MODEL % rows == 0
        return [pltpu.VMEM((D_MODEL, n_cols), BF16), pltpu.VMEM((W_STAGE_SLOTS, rows, n_cols), F32),
                pltpu.SemaphoreType.DMA((W_STAGE_SLOTS,))]
    if full:
        in_specs = [*tile_specs(lambda b, i: (b, i)), _layer_spec((1, D_MODEL), layer),
                    w_spec, gain_spec, gain_spec,
                    _const_spec(p["ones"].shape), tab_spec, tab_spec, tab_spec,
                    _layer_spec(p["conv_w"].shape[1:], layer), _layer_spec((1, CONV_WIDTH), layer),
                    _layer_spec((CONV_WIDTH, D_MODEL), layer), _layer_spec(p["b_gate"].shape[1:], layer)]
        args = (x, x, x, mod, mod, p["norm_g"], p["w_in"], p["q_gain"], p["k_gain"], p["ones"], *rope_tabs,
                p["conv_w"], p["conv_b"], p["w_conv_br"], p["b_gate"])
        out_specs = [qt_spec, k_spec, vt_spec, row_spec, row_spec, row_spec]
        out_shape = [jax.ShapeDtypeStruct((bsz, ATTN_WIDTH, seq), BF16), k_shape, vt_shape,
                     jax.ShapeDtypeStruct((bsz, seq, ATTN_WIDTH), BF16),
                     jax.ShapeDtypeStruct((bsz, seq, D_MODEL), BF16),
                     jax.ShapeDtypeStruct((bsz, seq, D_MODEL), F32)]
        scratch = [pltpu.VMEM((tm + 2 * HALO, D_MODEL), BF16),
                   pltpu.VMEM((tm + 2 * HALO, V7X_MXU_COLS), F32),
                   pltpu.VMEM((tm, CONV_WIDTH), BF16), *weight_scratch(PROJ_WIDTH)]
    else:
        kv_cols = 2 * KV_WIDTH
        x_spec, _, _, shift_spec, scale_spec = tile_specs(lambda b, i: (b, i))
        in_specs = [x_spec, shift_spec, scale_spec, _layer_spec((1, D_MODEL), layer),
                    w_spec,
                    gain_spec, _const_spec(p["ones"].shape), tab_spec, tab_spec, tab_spec]
        args = (x, mod, mod, p["norm_g"], p["w_in"], p["k_gain"], p["ones"], *rope_tabs)
        out_specs = [k_spec, vt_spec]
        out_shape = [k_shape, vt_shape]
        scratch = [pltpu.VMEM((tm, D_MODEL), BF16), *weight_scratch(kv_cols)]
    return pl.pallas_call(
        kern,
        grid=(bsz, n_tiles),
        in_specs=in_specs,
        out_specs=out_specs,
        out_shape=out_shape,
        scratch_shapes=scratch,
        compiler_params=pltpu.CompilerParams(dimension_semantics=("arbitrary", "arbitrary"),
                                             vmem_limit_bytes=PROJ_VMEM_LIMIT),
        name="projection_full" if full else "projection_kv",
    )(*args)


def _attention_kernel(*refs, n_seg):
    slot0_ref, qt_ref = refs[0], refs[1]
    refs = refs[1:]
    k_refs, vt_refs = refs[1:1 + n_seg], refs[1 + n_seg:1 + 2 * n_seg]
    (sza_ref, g0_ref, t1_ref, x_ref, gate_ref, wab_ref, wout_ref, o_ref, attn_scr, st_scr) = refs[1 + 2 * n_seg:]
    slot0 = slot0_ref[0]
    tiles, ring0 = [], 0
    for seg in range(n_seg):
        seg_keys = k_refs[seg].shape[2]
        size = min(KEY_TILE, seg_keys)
        for start in range(0, seg_keys, size):
            tiles.append((seg, slice(start, start + size), slice(ring0 + start, ring0 + start + size)))
        ring0 += seg_keys
    n_tiles = len(tiles)
    group = N_Q_HEADS // N_KV_HEADS
    sublanes = 8
    slots = st_scr.shape[0]
    tq = st_scr.shape[2]
    units = [(sub, h) for sub in range(qt_ref.shape[2] // tq) for h in range(N_Q_HEADS)]

    def score_tile(g, t, m8):
        sub, h = units[g]
        seg, seg_rows, ring_rows = tiles[t]
        s = _dot(k_refs[seg][0, h // group, seg_rows, :],
                 qt_ref[0, h * HEAD_DIM:(h + 1) * HEAD_DIM, sub * tq:(sub + 1) * tq])
        st_scr[slot0 + g % slots, ring_rows, :] = s.astype(BF16)
        for r in range(s.shape[0] // sublanes):
            piece = s[r * sublanes:(r + 1) * sublanes]
            m8 = piece if m8 is None else jnp.maximum(m8, piece)
        return m8

    def pv_tile(g, t, m_row, acc):
        seg, seg_rows, ring_rows = tiles[t]
        kv = units[g][1] // group
        p = jnp.exp2(st_scr[slot0 + g % slots, ring_rows, :] - m_row)
        part = _dot(vt_refs[seg][0, kv, :, seg_rows], p)
        return part if acc is None else acc + part

    def merge_stages(sub):
        rows = slice(sub * tq, (sub + 1) * tq)
        state = {}

        def branch():
            state["br"] = _dot(attn_scr[sub], wab_ref[...])

        def project():
            merged = g0_ref[0, rows, :].astype(F32) * state.pop("br") + t1_ref[0, rows, :]
            state["out"] = _dot(merged.astype(BF16), wout_ref[...])

        def residual():
            o_ref[0, rows, :] = x_ref[0, rows, :] + gate_ref[0] * state.pop("out")

        return [branch, project, residual]

    col_max = {}
    for g in range(min(SCORE_LOOKAHEAD, len(units))):
        m8 = None
        for t in range(n_tiles):
            m8 = score_tile(g, t, m8)
        col_max[g] = jnp.max(m8, axis=0, keepdims=True).astype(BF16)
    halves, pending = [], []
    for g, (sub, h) in enumerate(units):
        if pending:
            pending.pop(0)()
        acc, m8 = None, None
        ahead = g + SCORE_LOOKAHEAD
        m_row = col_max.pop(g)
        for t in range(n_tiles):
            if ahead < len(units):
                m8 = score_tile(ahead, t, m8)
            acc = pv_tile(g, t, m_row, acc)
        if ahead < len(units):
            col_max[ahead] = jnp.max(m8, axis=0, keepdims=True).astype(BF16)
        halves.append(acc[:HEAD_DIM] / acc[HEAD_DIM:HEAD_DIM + 1])
        if len(halves) == V7X_LANES // HEAD_DIM:
            j = h // len(halves)
            lanes = slice(j * V7X_LANES, (j + 1) * V7X_LANES)
            attn = jnp.concatenate(halves, axis=0).T
            gate = sza_ref[0, sub * tq:(sub + 1) * tq, lanes].astype(F32)
            attn_scr[sub, :, lanes] = (attn * gate).astype(BF16)
            halves = []
        if h == N_Q_HEADS - 1:
            pending += merge_stages(sub)
    for stage in pending:
        stage()


def _attention(qt, ks, vts, sza, g0, t1, x, mod, mod_row, layer, w_attn_br, w_out):
    bsz, seq, _ = x.shape
    sub_rows = min(QUERY_SUBTILE, seq)
    tq = min(ATTN_SUBTILES * sub_rows, seq)
    n_keys = sum(k.shape[2] for k in ks)
    mod_base = layer * MOD_ROWS
    k_specs = [pl.BlockSpec((1, N_KV_HEADS, k.shape[2], HEAD_DIM), lambda b, i: (b, 0, 0, 0)) for k in ks]
    vt_specs = [pl.BlockSpec((1, N_KV_HEADS, PV_ROWS, vt.shape[3]), lambda b, i: (b, 0, 0, 0)) for vt in vts]
    row_spec = pl.BlockSpec((1, tq, D_MODEL), lambda b, i: (b, i, 0))
    if mod_row is None:
        gate_spec = pl.BlockSpec((1, 1, D_MODEL), lambda b, i: (mod_base + b, 0, 2))
    else:
        gate_spec = pl.BlockSpec((1, 1, D_MODEL), lambda b, i: (mod_base + mod_row, 0, 2))
    return pl.pallas_call(
        functools.partial(_attention_kernel, n_seg=len(ks)),
        grid=(bsz, seq // tq),
        in_specs=[pl.BlockSpec(memory_space=pltpu.SMEM),
                  pl.BlockSpec((1, ATTN_WIDTH, tq), lambda b, i: (b, 0, i)), *k_specs, *vt_specs,
                  row_spec, row_spec, row_spec, row_spec, gate_spec,
                  _layer_spec(w_attn_br.shape[1:], layer), _layer_spec(w_out.shape[1:], layer)],
        out_specs=row_spec,
        out_shape=jax.ShapeDtypeStruct(x.shape, F32),
        scratch_shapes=[pltpu.VMEM((tq // sub_rows, sub_rows, ATTN_WIDTH), BF16),
                        pltpu.VMEM((SCORE_LOOKAHEAD + 1, n_keys, sub_rows), BF16)],
        compiler_params=pltpu.CompilerParams(dimension_semantics=("arbitrary", "arbitrary"),
                                             vmem_limit_bytes=ATTN_VMEM_LIMIT),
        name="attention_merge",
    )(jnp.zeros((1,), jnp.int32), qt, *ks, *vts, sza, g0, t1, x, mod, w_attn_br, w_out)


def _rope_tables(n_tokens):
    t = np.arange(n_tokens)
    row = (t // GRID_W).astype(np.float32)
    col = (t % GRID_W).astype(np.float32)
    half = HEAD_DIM // 2
    inv = (np.float32(ROPE_THETA) ** (-np.arange(0, half, 2, dtype=np.float32) / np.float32(half))).astype(np.float32)
    ang_r = row[:, None] * inv[None, :]
    ang_c = col[:, None] * inv[None, :]
    ang = np.concatenate([ang_r, ang_r, ang_c, ang_c], axis=-1).astype(np.float64)
    cos, sin = np.cos(ang), np.sin(ang)
    first = (np.arange(HEAD_DIM) % half) < half // 2
    sin_lo = np.where(first[None, :], -sin, 0.0)
    sin_hi = np.where(first[None, :], 0.0, sin)
    reps = V7X_LANES // HEAD_DIM
    return tuple(jnp.asarray(np.tile(a, (1, reps)), dtype=F32) for a in (cos, sin_lo, sin_hi))


def _identity_rope_tables(n_tokens):
    one = jnp.ones((n_tokens, V7X_LANES), F32)
    zero = jnp.zeros((n_tokens, V7X_LANES), F32)
    return one, zero, zero


def _head_sum_matrix():
    idx = np.arange(V7X_MXU_COLS) // HEAD_DIM
    return jnp.asarray(idx[:, None] == idx[None, :], dtype=BF16)


def kernel(x, c, ctx, c_ctx, norm_g, w_ada, b_ada, w_in, q_norm_g, k_norm_g, conv_w, conv_b, w_attn_br, w_conv_br,
           b_gate, w_out):
    depth = w_ada.shape[0]
    bsz, seq, _ = x.shape
    ctx_len = ctx.shape[1]
    assert bsz + 1 <= MOD_ROWS and w_ada.shape[2] % MOD_COLS == 0
    assert OFF_K % V7X_LANES == 0
    assert seq % PROJ_ROW_TILE == 0 and seq % (ATTN_SUBTILES * QUERY_SUBTILE) == 0 and seq % KEY_TILE == 0
    assert ctx_len <= min(PROJ_ROW_TILE, QUERY_SUBTILE, KEY_TILE) and ctx_len % V7X_LANES == 0

    cond = jnp.concatenate([c, c_ctx[None, :], jnp.zeros((MOD_ROWS - bsz - 1, D_MODEL), F32)], axis=0)
    mod = _modulation(cond, w_ada, b_ada).reshape(depth * MOD_ROWS, 1, 3 * D_MODEL)
    ctx_row = bsz

    rope_x = _rope_tables(seq)
    rope_c = _identity_rope_tables(ctx_len)
    reps = V7X_MXU_COLS // HEAD_DIM
    p = dict(
        norm_g=norm_g.reshape(depth, 1, D_MODEL),
        w_in=w_in,
        q_gain=jnp.tile(q_norm_g, (1, reps)).reshape(depth, 1, V7X_MXU_COLS),
        k_gain=jnp.tile(k_norm_g, (1, reps)).reshape(depth, 1, V7X_MXU_COLS),
        ones=_head_sum_matrix(),
        conv_w=conv_w,
        conv_b=conv_b.reshape(depth, 1, CONV_WIDTH),
        w_conv_br=w_conv_br.astype(BF16),
        b_gate=b_gate,
    )
    wab = w_attn_br.astype(BF16)
    wo = w_out.astype(BF16)

    for l in range(depth):
        last = l == depth - 1
        qx, ktx, vx, szax, g0x, t1x = _projection(x, mod, None, l, p, rope_x, rope=True, full=True)
        if last:
            ktc, vc = _projection(ctx, mod, ctx_row, l, p, rope_c, rope=False, full=False)
        else:
            qc, ktc, vc, szac, g0c, t1c = _projection(ctx, mod, ctx_row, l, p, rope_c, rope=False, full=True)
        x_new = _attention(qx, [ktc, ktx], [vc, vx], szax, g0x, t1x, x, mod, None, l, wab, wo)
        if not last:
            ctx = _attention(qc, [ktc], [vc], szac, g0c, t1c, ctx, mod, ctx_row, l, wab, wo)
        x = x_new
    return x
```
